```python
import jax
import jax.numpy as jnp
from jax import lax
import numpy as np


D_MODEL = 4096
BATCH = 8
SEQ = 2048
DEPTH = 4

GRID_W = 64
CTX_LEN = 256
N_MIXERS = 4
N_MOD = 6
EPS = 1e-6
NEG_INF = -1e30

POOL_WINDOWS = (2, 4, 8, 16)
POOL_GROUPS = 4
POOL_GROUP_DIM = D_MODEL // POOL_GROUPS

CONV_W = 3

HEAD_DIM = 128
N_HEADS = D_MODEL // HEAD_DIM
N_KV_HEADS = N_HEADS // 4
KV_GROUP = N_HEADS // N_KV_HEADS
WINDOW = 128
Q_BLOCK = 128
BAND_SIDE = WINDOW // Q_BLOCK
ROPE_BASE = 10000.0

NA_KH_MAX = 8
NA_KW = 16
NA_BLOCK_W = 16
NA_SPAN_W = NA_BLOCK_W + NA_KW

N_EXPERTS = 16
EXPERT_FF = 384
EC_CAPACITY_FACTOR = 2

kernel_name = 'hybrid_diffusion_backbone'


def rmsnorm(x, gain):
    xf = x.astype(jnp.float32)
    y = xf * lax.rsqrt(jnp.mean(xf * xf, axis=-1, keepdims=True) + EPS)
    return (y * gain.astype(jnp.float32)).astype(x.dtype)


def modulate(h, shift, scale):
    return h * (1 + scale) + shift


def rope_axis(x, pos):
    half = x.shape[-1] // 2
    inv_freq = ROPE_BASE ** (-jnp.arange(half, dtype=jnp.float32) / half)
    ang = pos.astype(jnp.float32)[:, None] * inv_freq[None, :]
    cos = jnp.cos(ang)[:, None, :].astype(x.dtype)
    sin = jnp.sin(ang)[:, None, :].astype(x.dtype)
    x1, x2 = x[..., :half], x[..., half:]
    return jnp.concatenate([x1 * cos - x2 * sin, x2 * cos + x1 * sin], axis=-1)


def rope_2d(x):
    t = jnp.arange(x.shape[1])
    r = HEAD_DIM // 2
    return jnp.concatenate([rope_axis(x[..., :r], t // GRID_W), rope_axis(x[..., r:], t % GRID_W)], axis=-1)


def softmax_with_sink(s, sink):
    sink_col = jnp.broadcast_to(sink, s.shape[:-1] + (1,))
    return jax.nn.softmax(jnp.concatenate([s, sink_col], axis=-1), axis=-1)[..., :-1]


def pool_mixer(h, w_pool, scale):
    bsz, n, _ = h.shape
    hf = h.astype(jnp.float32)
    csum = jnp.pad(jnp.cumsum(hf, axis=1), ((0, 0), (1, 0), (0, 0)))
    t = np.arange(n)
    groups = []
    for g, w in enumerate(POOL_WINDOWS):
        lo = np.clip(t - w // 2, 0, n)
        hi = np.clip(t + w // 2, 0, n)
        cnt = (hi - lo).astype(np.float32)[None, :, None]
        sl = slice(g * POOL_GROUP_DIM, (g + 1) * POOL_GROUP_DIM)
        cs = csum[:, :, sl]
        groups.append((cs[:, hi] - cs[:, lo]) / cnt - hf[:, :, sl])
    pooled = jnp.stack(groups, axis=2).astype(h.dtype)
    y = jnp.einsum('bngc,gcd->bngd', pooled, w_pool).reshape(bsz, n, D_MODEL)
    return y * scale


def short_conv_mixer(h, w_in, conv_w, w_out):
    n = h.shape[1]
    b_gate, c_gate, v = jnp.split(h @ w_in, 3, axis=-1)
    u = jnp.pad(c_gate * v, ((0, 0), (CONV_W // 2, CONV_W // 2), (0, 0)))
    conv = u[:, 0:n] * conv_w[0]
    for k in range(1, CONV_W):
        conv = conv + u[:, k:k + n] * conv_w[k]
    return (b_gate * conv) @ w_out


def window_gqa_mixer(hl, hc, w_qkv, w_o, sink, need_ctx):
    bsz, s_len, _ = hl.shape
    n_ctx = hc.shape[1]
    kv_dim = N_KV_HEADS * HEAD_DIM
    scale = HEAD_DIM ** -0.5
    qkv_l = hl @ w_qkv
    ql = rope_2d(qkv_l[..., :D_MODEL].reshape(bsz, s_len, N_HEADS, HEAD_DIM))
    ql = ql.reshape(bsz, s_len, N_KV_HEADS, KV_GROUP, HEAD_DIM)
    kl = rope_2d(qkv_l[..., D_MODEL:D_MODEL + kv_dim].reshape(bsz, s_len, N_KV_HEADS, HEAD_DIM))
    vl = qkv_l[..., D_MODEL + kv_dim:].reshape(bsz, s_len, N_KV_HEADS, HEAD_DIM)
    kv_c = hc @ w_qkv[:, D_MODEL:]
    kc = kv_c[..., :kv_dim].reshape(bsz, n_ctx, N_KV_HEADS, HEAD_DIM)
    vc = kv_c[..., kv_dim:].reshape(bsz, n_ctx, N_KV_HEADS, HEAD_DIM)
    sink_f = sink.astype(jnp.float32).reshape(N_KV_HEADS, KV_GROUP, 1, 1)

    nb = s_len // Q_BLOCK
    band = (2 * BAND_SIDE + 1) * Q_BLOCK
    pad = ((0, 0), (BAND_SIDE * Q_BLOCK, BAND_SIDE * Q_BLOCK), (0, 0), (0, 0))

    def banded(t):
        tp = jnp.pad(t, pad).reshape(bsz, nb + 2 * BAND_SIDE, Q_BLOCK, N_KV_HEADS, HEAD_DIM)
        tb = jnp.concatenate([tp[:, o:o + nb] for o in range(2 * BAND_SIDE + 1)], axis=2)
        return tb.transpose(1, 0, 2, 3, 4)

    qb = ql.reshape(bsz, nb, Q_BLOCK, N_KV_HEADS, KV_GROUP, HEAD_DIM).transpose(1, 0, 2, 3, 4, 5)

    def block(args):
        q_j, k_j, v_j, j = args
        s_win = jnp.einsum('bqhgd,blhd->bhgql', q_j, k_j, preferred_element_type=jnp.float32) * scale
        qpos = j * Q_BLOCK + jnp.arange(Q_BLOCK)
        kpos = (j - BAND_SIDE) * Q_BLOCK + jnp.arange(band)
        valid = (jnp.abs(qpos[:, None] - kpos[None, :]) <= WINDOW) & ((kpos >= 0) & (kpos < s_len))[None, :]
        s_win = jnp.where(valid, s_win, NEG_INF)
        s_ctx = jnp.einsum('bqhgd,bchd->bhgqc', q_j, kc, preferred_element_type=jnp.float32) * scale
        p = softmax_with_sink(jnp.concatenate([s_win, s_ctx], axis=-1), sink_f).astype(v_j.dtype)
        return (jnp.einsum('bhgql,blhd->bqhgd', p[..., :band], v_j)
                + jnp.einsum('bhgqc,bchd->bqhgd', p[..., band:], vc))

    o_lat = lax.map(block, (qb, banded(kl), banded(vl), jnp.arange(nb)))
    yl = o_lat.transpose(1, 0, 2, 3, 4, 5).reshape(bsz, s_len, D_MODEL) @ w_o
    if not need_ctx:
        return yl, None
    qc = (hc @ w_qkv[:, :D_MODEL]).reshape(bsz, n_ctx, N_KV_HEADS, KV_GROUP, HEAD_DIM)
    s = jnp.einsum('bqhgd,bchd->bhgqc', qc, kc, preferred_element_type=jnp.float32) * scale
    p = softmax_with_sink(s, sink_f).astype(vc.dtype)
    yc = jnp.einsum('bhgqc,bchd->bqhgd', p, vc).reshape(bsz, n_ctx, D_MODEL) @ w_o
    return yl, yc


def neighbourhood_mixer(hl, hc, w_qkv, w_o, rel_bias, need_ctx):
    bsz, s_len, _ = hl.shape
    n_ctx = hc.shape[1]
    rows = s_len // GRID_W
    kh = min(NA_KH_MAX, rows)
    nbw = GRID_W // NA_BLOCK_W
    span = kh * NA_SPAN_W
    scale = HEAD_DIM ** -0.5
    ql, kl, vl = jnp.split(hl @ w_qkv, 3, axis=-1)
    grid = (bsz, rows, GRID_W, N_HEADS, HEAD_DIM)
    qg, kg, vg = ql.reshape(grid), kl.reshape(grid), vl.reshape(grid)
    kc, vc = jnp.split(hc @ w_qkv[:, D_MODEL:], 2, axis=-1)
    kc = kc.reshape(bsz, n_ctx, N_HEADS, HEAD_DIM)
    vc = vc.reshape(bsz, n_ctx, N_HEADS, HEAD_DIM)

    qcol = np.arange(GRID_W).reshape(nbw, NA_BLOCK_W)
    kc0 = np.clip(np.arange(nbw) * NA_BLOCK_W - NA_KW // 2, 0, GRID_W - NA_SPAN_W)
    kcol = kc0[:, None] + np.arange(NA_SPAN_W)[None, :]
    c0 = np.clip(qcol - NA_KW // 2, 0, GRID_W - NA_KW)
    col_ok = (kcol[:, None, :] >= c0[:, :, None]) & (kcol[:, None, :] < c0[:, :, None] + NA_KW)
    mask = np.broadcast_to(col_ok[:, :, None, :], (nbw, NA_BLOCK_W, kh, NA_SPAN_W)).reshape(nbw, NA_BLOCK_W, span)
    dc_idx = np.clip(kcol[:, None, :] - qcol[:, :, None] + NA_KW - 1, 0, 2 * NA_KW - 2)
    bias_cols = rel_bias.astype(jnp.float32)[:, :, dc_idx]

    def row(r):
        r0 = jnp.clip(r - kh // 2, 0, rows - kh)

        def gather_keys(t):
            t_rows = lax.dynamic_slice_in_dim(t, r0, kh, axis=1)
            t_blk = t_rows[:, :, kcol]
            return t_blk.transpose(0, 2, 1, 3, 4, 5).reshape(bsz, nbw, span, N_HEADS, HEAD_DIM)

        k_blk, v_blk = gather_keys(kg), gather_keys(vg)
        q_row = lax.dynamic_index_in_dim(qg, r, axis=1, keepdims=False).reshape(bsz, nbw, NA_BLOCK_W, N_HEADS, HEAD_DIM)
        dr_idx = r0 + jnp.arange(kh) - r + NA_KH_MAX - 1
        bias = bias_cols[:, dr_idx].transpose(0, 2, 3, 1, 4).reshape(N_HEADS, nbw, NA_BLOCK_W, span)
        s_nb = jnp.einsum('bnqhd,bnkhd->bhnqk', q_row, k_blk, preferred_element_type=jnp.float32) * scale
        s_nb = jnp.where(mask, s_nb + bias, NEG_INF)
        s_ctx = jnp.einsum('bnqhd,bchd->bhnqc', q_row, kc, preferred_element_type=jnp.float32) * scale
        p = jax.nn.softmax(jnp.concatenate([s_nb, s_ctx], axis=-1), axis=-1).astype(vg.dtype)
        o = (jnp.einsum('bhnqk,bnkhd->bnqhd', p[..., :span], v_blk)
             + jnp.einsum('bhnqc,bchd->bnqhd', p[..., span:], vc))
        return o.reshape(bsz, GRID_W, D_MODEL)

    o_lat = lax.map(row, jnp.arange(rows))
    yl = o_lat.transpose(1, 0, 2, 3).reshape(bsz, s_len, D_MODEL) @ w_o
    if not need_ctx:
        return yl, None
    qc = (hc @ w_qkv[:, :D_MODEL]).reshape(bsz, n_ctx, N_HEADS, HEAD_DIM)
    s = jnp.einsum('bqhd,bchd->bhqc', qc, kc, preferred_element_type=jnp.float32) * scale
    p = jax.nn.softmax(s, axis=-1).astype(vc.dtype)
    yc = jnp.einsum('bhqc,bchd->bqhd', p, vc).reshape(bsz, n_ctx, D_MODEL) @ w_o
    return yl, yc


def ec_moe(h, w_router, w_gate, w_up, w_down):
    bsz, n, d = h.shape
    cap = EC_CAPACITY_FACTOR * n // N_EXPERTS
    logits = jnp.einsum('bnd,de->ben', h, w_router, preferred_element_type=jnp.float32)
    aff = jax.nn.softmax(logits, axis=1)
    gate, idx = lax.top_k(aff, cap)
    xs = jax.vmap(lambda hb, ib: hb[ib])(h, idx)
    hid = jax.nn.silu(jnp.einsum('becd,edf->becf', xs, w_gate)) * jnp.einsum('becd,edf->becf', xs, w_up)
    y = jnp.einsum('becf,efd->becd', hid, w_down) * gate[..., None].astype(h.dtype)
    return jax.vmap(lambda yb, ib: jnp.zeros((n, d), yb.dtype).at[ib.reshape(-1)].add(yb.reshape(-1, d)))(y, idx)


def setup_inputs(seed: int = 0) -> dict:
    key = jax.random.key(seed)
    ks = jax.random.split(key, 24)
    n_a, n_b, n_c, n_d = [len(range(m, DEPTH, N_MIXERS)) for m in range(N_MIXERS)]
    d = D_MODEL
    qkv_gqa = D_MODEL + 2 * N_KV_HEADS * HEAD_DIM

    def nrm(i, shape, scale):
        return jax.random.normal(ks[i], shape, jnp.float32) * scale

    return {
        'x': nrm(0, (BATCH, SEQ, d), 1.0),
        'c': nrm(1, (BATCH, d), 1.0),
        'ctx': nrm(2, (BATCH, CTX_LEN, d), 1.0),
        'c_ctx': nrm(3, (d,), 1.0),
        'norm1_gain': 1.0 + nrm(4, (DEPTH, d), 0.1),
        'norm2_gain': 1.0 + nrm(5, (DEPTH, d), 0.1),
        'final_gain': 1.0 + nrm(6, (d,), 0.1),
        'w_ada': nrm(7, (DEPTH, d, N_MOD * d), d ** -0.5),
        'b_ada': nrm(8, (DEPTH, N_MOD * d), 0.01),
        'pool_w': nrm(9, (n_a, POOL_GROUPS, POOL_GROUP_DIM, POOL_GROUP_DIM), POOL_GROUP_DIM ** -0.5),
        'pool_scale': 1.0 + nrm(10, (n_a, d), 0.1),
        'conv_w_in': nrm(11, (n_b, d, 3 * d), d ** -0.5),
        'conv_w': nrm(12, (n_b, CONV_W, d), CONV_W ** -0.5),
        'conv_w_out': nrm(13, (n_b, d, d), d ** -0.5),
        'swa_w_qkv': nrm(14, (n_c, d, qkv_gqa), d ** -0.5),
        'swa_w_o': nrm(15, (n_c, d, d), d ** -0.5),
        'swa_sink': nrm(16, (n_c, N_HEADS), 1.0),
        'na_w_qkv': nrm(17, (n_d, d, 3 * d), d ** -0.5),
        'na_w_o': nrm(18, (n_d, d, d), d ** -0.5),
        'na_rel_bias': nrm(19, (n_d, N_HEADS, 2 * NA_KH_MAX - 1, 2 * NA_KW - 1), 0.2),
        'moe_router': nrm(20, (DEPTH, d, N_EXPERTS), d ** -0.5),
        'moe_w_gate': nrm(21, (DEPTH, N_EXPERTS, d, EXPERT_FF), d ** -0.5),
        'moe_w_up': nrm(22, (DEPTH, N_EXPERTS, d, EXPERT_FF), d ** -0.5),
        'moe_w_down': nrm(23, (DEPTH, N_EXPERTS, EXPERT_FF, d), EXPERT_FF ** -0.5),
    }


def reference(x, c, ctx, c_ctx, norm1_gain, norm2_gain, final_gain, w_ada, b_ada,
              pool_w, pool_scale, conv_w_in, conv_w, conv_w_out,
              swa_w_qkv, swa_w_o, swa_sink, na_w_qkv, na_w_o, na_rel_bias,
              moe_router, moe_w_gate, moe_w_up, moe_w_down):
    silu_c = jax.nn.silu(c)
    silu_cc = jax.nn.silu(c_ctx)
    for i in range(DEPTH):
        m, j = i % N_MIXERS, i // N_MIXERS
        need_ctx = i < DEPTH - 1
        mod_l = [t[:, None, :] for t in jnp.split(silu_c @ w_ada[i] + b_ada[i], N_MOD, axis=-1)]
        mod_c = jnp.split(silu_cc @ w_ada[i] + b_ada[i], N_MOD, axis=-1)
        hl = modulate(rmsnorm(x, norm1_gain[i]), mod_l[0], mod_l[1])
        if need_ctx or m >= 2:
            hc = modulate(rmsnorm(ctx, norm1_gain[i]), mod_c[0], mod_c[1])
        if m == 0:
            yl = pool_mixer(hl, pool_w[j], pool_scale[j])
            yc = pool_mixer(hc, pool_w[j], pool_scale[j]) if need_ctx else None
        elif m == 1:
            yl = short_conv_mixer(hl, conv_w_in[j], conv_w[j], conv_w_out[j])
            yc = short_conv_mixer(hc, conv_w_in[j], conv_w[j], conv_w_out[j]) if need_ctx else None
        elif m == 2:
            yl, yc = window_gqa_mixer(hl, hc, swa_w_qkv[j], swa_w_o[j], swa_sink[j], need_ctx)
        else:
            yl, yc = neighbourhood_mixer(hl, hc, na_w_qkv[j], na_w_o[j], na_rel_bias[j], need_ctx)
        x = x + mod_l[2] * yl
        h2 = modulate(rmsnorm(x, norm2_gain[i]), mod_l[3], mod_l[4])
        x = x + mod_l[5] * ec_moe(h2, moe_router[i], moe_w_gate[i], moe_w_up[i], moe_w_down[i])
        if need_ctx:
            ctx = ctx + mod_c[2] * yc
            h2c = modulate(rmsnorm(ctx, norm2_gain[i]), mod_c[3], mod_c[4])
            ctx = ctx + mod_c[5] * ec_moe(h2c, moe_router[i], moe_w_gate[i], moe_w_up[i], moe_w_down[i])
    return rmsnorm(x, final_gain)
```

```python
import functools
from typing import NamedTuple

import numpy as np
import jax
import jax.numpy as jnp
from jax import lax
from jax.experimental import pallas as pl
from jax.experimental.pallas import tpu as pltpu

GRID_W = 64
HEAD_DIM = 128
KV_GROUP = 4
N_MOD = 6
EPS = 1e-6
NEG_INF = -1e30
POOL_WINDOWS = (2, 4, 8, 16)
CONV_W = 3
WINDOW = 128
Q_BLOCK = 128
ROPE_BASE = 10000.0
NA_KH_MAX = 8
NA_KW = 16
NA_ROWS_PER_BLOCK = 4
EC_CAPACITY_FACTOR = 2
LANE = 128
MOD_ROWS = 16
V7X_VMEM_LIMIT = 56 * 1024 * 1024

BF16 = jnp.bfloat16
F32 = jnp.float32


class Tiles(NamedTuple):
    norm_rows: int = 256
    mm_rows: int = 1024
    mm_cols: int = 1024
    res_cols: int = 512
    conv_cols: int = 512
    ada_cols: int = 512
    pool_cols: int = 512
    pool_chunk: int = 256
    gate_cols: int = 512
    swa_kv_heads: int = 2
    na_heads: int = 4
    moe_tokens: int = 256


def _params(vmem=V7X_VMEM_LIMIT):
    return pltpu.CompilerParams(vmem_limit_bytes=vmem)


def _dot(a, b):
    return jnp.dot(a, b, preferred_element_type=F32)


def _dot_nt(a, b):
    return lax.dot_general(a, b, (((1,), (1,)), ((), ())), preferred_element_type=F32)


def _ada_body(c_ref, w_ref, b_ref, o_ref):
    cvec = c_ref[...]
    a = (cvec * jax.nn.sigmoid(cvec)).astype(BF16)
    o_ref[...] = _dot(a, w_ref[...].astype(BF16)) + b_ref[...]


def _ada_call(cc, w_ada, b_ada, t):
    depth, d, nd = w_ada.shape
    bn = t.ada_cols
    return pl.pallas_call(
        _ada_body,
        grid=(depth, nd // bn),
        in_specs=[
            pl.BlockSpec((MOD_ROWS, d), lambda l, j: (0, 0)),
            pl.BlockSpec((None, d, bn), lambda l, j: (l, 0, j)),
            pl.BlockSpec((None, 1, bn), lambda l, j: (l, 0, j)),
        ],
        out_specs=pl.BlockSpec((None, MOD_ROWS, bn), lambda l, j: (l, 0, j)),
        out_shape=jax.ShapeDtypeStruct((depth, MOD_ROWS, nd), F32),
        compiler_params=_params(),
        name="adaln",
    )(cc, w_ada, b_ada.reshape(depth, 1, nd))


def _mod_spec(layer, k, cols, lat, bm, col_map=None):
    seq, bsz = lat
    sample = lambda i: jnp.minimum((i * bm) // seq, bsz)
    if col_map is None:
        return pl.BlockSpec((None, None, None, 1, cols), lambda i: (layer, sample(i), k, 0, 0))
    return pl.BlockSpec((None, None, None, 1, cols),
                        lambda j, i: (layer, sample(i), k, 0, col_map(j)))


def _norm_body(*refs, modulate, router, n_experts):
    x_ref, g_ref = refs[0], refs[1]
    pos = 2
    x = x_ref[...]
    y = x * lax.rsqrt(jnp.mean(x * x, axis=-1, keepdims=True) + EPS)
    y = y * g_ref[...]
    if modulate:
        sh_ref, sc_ref = refs[pos], refs[pos + 1]
        pos += 2
        y = y * (1 + sc_ref[...]) + sh_ref[...]
    if router:
        wr_ref = refs[pos]
        pos += 1
    o_ref = refs[pos]
    o_ref[...] = y.astype(o_ref.dtype)
    if router:
        aff_ref = refs[pos + 1]
        logits = _dot(y.astype(BF16), wr_ref[...])
        lane = lax.broadcasted_iota(jnp.int32, logits.shape, 1)
        logits = jnp.where(lane < n_experts, logits, NEG_INF)
        m = jnp.max(logits, axis=-1, keepdims=True)
        p = jnp.exp(logits - m)
        aff_ref[...] = p / jnp.sum(p, axis=-1, keepdims=True)


def _norm_call(x, gain, mods5, layer, k_shift, k_scale, seq, t, *, out_dtype, n_rows,
               modulate=True, w_router=None, n_experts=0):
    rows, d = x.shape
    bm = t.norm_rows
    in_specs = [pl.BlockSpec((bm, d), lambda i: (i, 0)),
                pl.BlockSpec((1, d), lambda i: (0, 0))]
    args = [x, gain.reshape(1, d)]
    if modulate:
        in_specs += [_mod_spec(layer, k_shift, d, seq, bm), _mod_spec(layer, k_scale, d, seq, bm)]
        args += [mods5, mods5]
    out_shape = [jax.ShapeDtypeStruct((n_rows, d), out_dtype)]
    out_specs = [pl.BlockSpec((bm, d), lambda i: (i, 0))]
    router = w_router is not None
    if router:
        in_specs.append(pl.BlockSpec((d, LANE), lambda i: (0, 0)))
        args.append(w_router)
        out_shape.append(jax.ShapeDtypeStruct((n_rows, LANE), F32))
        out_specs.append(pl.BlockSpec((bm, LANE), lambda i: (i, 0)))
    res = pl.pallas_call(
        functools.partial(_norm_body, modulate=modulate, router=router, n_experts=n_experts),
        grid=(n_rows // bm,),
        in_specs=in_specs, out_specs=out_specs, out_shape=out_shape,
        compiler_params=_params(),
        name="norm_mod",
    )(*args)
    return res if router else res[0]


def _mm_plain_body(x_ref, w_ref, o_ref):
    o_ref[...] = _dot(x_ref[...], w_ref[...]).astype(o_ref.dtype)


def _mm_plain(x, w, t, *, col_off=0, n_cols=None, out_dtype=BF16):
    rows, k = x.shape
    n_cols = w.shape[1] - col_off if n_cols is None else n_cols
    bm, bn = t.mm_rows, t.mm_cols
    off = col_off // bn
    return pl.pallas_call(
        _mm_plain_body,
        grid=(n_cols // bn, rows // bm),
        in_specs=[pl.BlockSpec((bm, k), lambda j, i: (i, 0)),
                  pl.BlockSpec((k, bn), lambda j, i: (0, off + j))],
        out_specs=pl.BlockSpec((bm, bn), lambda j, i: (i, j)),
        out_shape=jax.ShapeDtypeStruct((rows, n_cols), out_dtype),
        compiler_params=_params(),
        name="mm_plain",
    )(x, w)


def _rope_rotate(blk, lane):
    fwd = pltpu.roll(blk, LANE - HEAD_DIM // 4, 1)
    bwd = pltpu.roll(blk, HEAD_DIM // 4, 1)
    return jnp.where(lane % (HEAD_DIM // 2) < HEAD_DIM // 4, fwd, bwd)


def _mm_rope_body(x_ref, w_ref, cos_ref, sin_ref, o_ref, *, n_rope_blocks):
    acc = _dot(x_ref[...], w_ref[...])
    j = pl.program_id(0)

    @pl.when(j < n_rope_blocks)
    def _():
        cos, sin = cos_ref[...], sin_ref[...]
        lane = lax.broadcasted_iota(jnp.int32, cos.shape, 1)
        for hd in range(acc.shape[1] // HEAD_DIM):
            sl = slice(hd * HEAD_DIM, (hd + 1) * HEAD_DIM)
            blk = acc[:, sl]
            o_ref[:, sl] = (blk * cos + _rope_rotate(blk, lane) * sin).astype(o_ref.dtype)

    @pl.when(j >= n_rope_blocks)
    def _():
        o_ref[...] = acc.astype(o_ref.dtype)


def _mm_rope(x, w, cos, sin, n_rope_cols, t):
    rows, k = x.shape
    n = w.shape[1]
    bm, bn = t.mm_rows, t.mm_cols
    return pl.pallas_call(
        functools.partial(_mm_rope_body, n_rope_blocks=n_rope_cols // bn),
        grid=(n // bn, rows // bm),
        in_specs=[pl.BlockSpec((bm, k), lambda j, i: (i, 0)),
                  pl.BlockSpec((k, bn), lambda j, i: (0, j)),
                  pl.BlockSpec((bm, HEAD_DIM), lambda j, i: (i, 0)),
                  pl.BlockSpec((bm, HEAD_DIM), lambda j, i: (i, 0))],
        out_specs=pl.BlockSpec((bm, bn), lambda j, i: (i, j)),
        out_shape=jax.ShapeDtypeStruct((rows, n), BF16),
        compiler_params=_params(),
        name="mm_rope",
    )(x, w, cos, sin)


def _mm_conv_in_body(x_ref, wb_ref, wc_ref, wv_ref, b_ref, u_ref):
    x = x_ref[...]
    b_ref[...] = _dot(x, wb_ref[...]).astype(b_ref.dtype)
    u_ref[...] = (_dot(x, wc_ref[...]) * _dot(x, wv_ref[...])).astype(u_ref.dtype)


def _mm_conv_in(x, w_in, t):
    rows, k = x.shape
    d = w_in.shape[1] // 3
    bm, bn = t.mm_rows, t.conv_cols
    nb = d // bn
    wspec = lambda part: pl.BlockSpec((k, bn), lambda j, i: (0, part * nb + j))
    ospec = pl.BlockSpec((bm, bn), lambda j, i: (i, j))
    return pl.pallas_call(
        _mm_conv_in_body,
        grid=(nb, rows // bm),
        in_specs=[pl.BlockSpec((bm, k), lambda j, i: (i, 0)), wspec(0), wspec(1), wspec(2)],
        out_specs=[ospec, ospec],
        out_shape=[jax.ShapeDtypeStruct((rows, d), BF16)] * 2,
        compiler_params=_params(),
        name="mm_conv_in",
    )(x, w_in, w_in, w_in)


def _mm_res_body(*refs, col_scale):
    x_ref, w_ref, res_ref, mod_ref = refs[:4]
    o_ref = refs[-1]
    acc = _dot(x_ref[...], w_ref[...])
    if col_scale:
        acc = acc * refs[4][...]
    o_ref[...] = res_ref[...] + mod_ref[...] * acc


def _mm_res(x, w, res, mods5, layer, k_gate, seq, t, *, n_rows, col_scale=None, grouped=False):
    k = x.shape[1]
    d = res.shape[1]
    bm = t.mm_rows
    if grouped:
        kg = w.shape[1]
        bn = kg
        x_spec = pl.BlockSpec((bm, kg), lambda j, i: (i, j))
        w_spec = pl.BlockSpec((None, kg, kg), lambda j, i: (j, 0, 0))
    else:
        bn = t.res_cols
        x_spec = pl.BlockSpec((bm, k), lambda j, i: (i, 0))
        w_spec = pl.BlockSpec((k, bn), lambda j, i: (0, j))
    in_specs = [x_spec, w_spec,
                pl.BlockSpec((bm, bn), lambda j, i: (i, j)),
                _mod_spec(layer, k_gate, bn, seq, bm, col_map=lambda j: j)]
    args = [x, w, res, mods5]
    if col_scale is not None:
        in_specs.append(pl.BlockSpec((1, bn), lambda j, i: (0, j)))
        args.append(col_scale.reshape(1, d))
    return pl.pallas_call(
        functools.partial(_mm_res_body, col_scale=col_scale is not None),
        grid=(d // bn, n_rows // bm),
        in_specs=in_specs,
        out_specs=pl.BlockSpec((bm, bn), lambda j, i: (i, j)),
        out_shape=jax.ShapeDtypeStruct(res.shape, F32),
        input_output_aliases={2: 0},
        compiler_params=_params(),
        name="mm_res",
    )(*args)


_POOL_HALO = 16


def _pool_body(h_ref, o_ref, pad_ref, *, seq, chunk, groups_per_col):
    cols = h_ref.shape[1]
    zeros = jnp.zeros((_POOL_HALO, cols), F32)
    pad_ref[pl.ds(0, _POOL_HALO), :] = zeros
    pad_ref[pl.ds(_POOL_HALO + seq, _POOL_HALO), :] = zeros
    pad_ref[pl.ds(_POOL_HALO, seq), :] = h_ref[...]
    win_rows = chunk + 2 * _POOL_HALO
    group = pl.program_id(1) // groups_per_col

    for g, w in enumerate(POOL_WINDOWS):
        @pl.when(group == g)
        def _(w=w):
            def body(c, carry):
                start = pl.multiple_of(c * chunk, chunk)
                win = pad_ref[pl.ds(start, win_rows), :]
                s, width = win, 1
                while width < w:
                    s = s + pltpu.roll(s, width, 0)
                    width *= 2
                lead = w // 2 - 1
                if lead:
                    s = pltpu.roll(s, win_rows - lead, 0)
                tok = start + lax.broadcasted_iota(jnp.int32, (chunk, 1), 0)
                cnt = (jnp.minimum(tok + w // 2, seq) - jnp.maximum(tok - w // 2, 0)).astype(F32)
                centre = slice(_POOL_HALO, _POOL_HALO + chunk)
                o_ref[pl.ds(start, chunk), :] = (s[centre] / cnt - win[centre]).astype(o_ref.dtype)
                return carry
            lax.fori_loop(0, seq // chunk, body, 0)


def _pool_call(h, seq, n_seq, row_off, t):
    rows, d = h.shape
    bc = t.pool_cols
    chunk = min(t.pool_chunk, seq)
    group_cols = d // len(POOL_WINDOWS)
    off = row_off // seq
    spec = pl.BlockSpec((seq, bc), lambda b, j: (off + b, j))
    return pl.pallas_call(
        functools.partial(_pool_body, seq=seq, chunk=chunk, groups_per_col=group_cols // bc),
        grid=(n_seq, d // bc),
        in_specs=[spec],
        out_specs=pl.BlockSpec((seq, bc), lambda b, j: (b, j)),
        out_shape=jax.ShapeDtypeStruct((n_seq * seq, d), BF16),
        scratch_shapes=[pltpu.VMEM((seq + 2 * _POOL_HALO, bc), F32)],
        compiler_params=_params(),
        name="pool",
    )(h)


def _conv_gate_body(b_ref, u_ref, w_ref, o_ref, *, seq):
    u = u_ref[...].astype(F32)
    row = lax.broadcasted_iota(jnp.int32, (seq, 1), 0)
    prev = jnp.where(row == 0, 0.0, pltpu.roll(u, 1, 0))
    nxt = jnp.where(row == seq - 1, 0.0, pltpu.roll(u, seq - 1, 0))
    conv = prev * w_ref[0:1, :] + u * w_ref[1:2, :] + nxt * w_ref[2:3, :]
    o_ref[...] = (b_ref[...].astype(F32) * conv).astype(o_ref.dtype)


def _conv_gate_call(bg, u, conv_w, seq, n_seq, row_off, t):
    rows, d = bg.shape
    bc = t.gate_cols
    off = row_off // seq
    spec = pl.BlockSpec((seq, bc), lambda b, j: (off + b, j))
    return pl.pallas_call(
        functools.partial(_conv_gate_body, seq=seq),
        grid=(n_seq, d // bc),
        in_specs=[spec, spec, pl.BlockSpec((CONV_W, bc), lambda b, j: (0, j))],
        out_specs=pl.BlockSpec((seq, bc), lambda b, j: (b, j)),
        out_shape=jax.ShapeDtypeStruct((n_seq * seq, d), BF16),
        compiler_params=_params(),
        name="conv_gate",
    )(bg, u, conv_w)


def _attn_body(*refs, n_band, group, heads, scale, has_bias, bias_per_head, has_sink):
    q_ref = refs[0]
    pos = 1
    k_band = refs[pos:pos + n_band]
    v_band = refs[pos + n_band:pos + 2 * n_band]
    pos += 2 * n_band
    kx_ref, vx_ref = refs[pos], refs[pos + 1]
    pos += 2
    bias_ref = sink_ref = None
    if has_bias:
        bias_ref = refs[pos]
        pos += 1
    if has_sink:
        sink_ref = refs[pos]
        pos += 1
    o_ref = refs[pos]
    kv_block = pl.program_id(2) if has_sink else 0

    for hh in range(heads):
        ksl = slice(hh * HEAD_DIM, (hh + 1) * HEAD_DIM)
        keys = [r[:, ksl] for r in k_band] + [kx_ref[:, ksl]]
        vals = [r[:, ksl] for r in v_band] + [vx_ref[:, ksl]]
        for g in range(group):
            qh = hh * group + g
            qsl = slice(qh * HEAD_DIM, (qh + 1) * HEAD_DIM)
            q = q_ref[:, qsl]
            scores = [_dot_nt(q, k) * scale for k in keys]
            if has_bias:
                tk = k_band[0].shape[0]
                for bi in range(n_band):
                    bsl = slice(bi * tk, (bi + 1) * tk)
                    bias = bias_ref[hh, :, bsl] if bias_per_head else bias_ref[0, :, bsl]
                    scores[bi] = scores[bi] + bias
            m = scores[0].max(axis=-1, keepdims=True)
            for s in scores[1:]:
                m = jnp.maximum(m, s.max(axis=-1, keepdims=True))
            if has_sink:
                sink = sink_ref[kv_block * heads * group + qh]
                m = jnp.maximum(m, sink)
            probs = [jnp.exp(s - m) for s in scores]
            denom = probs[0].sum(axis=-1, keepdims=True)
            for p in probs[1:]:
                denom = denom + p.sum(axis=-1, keepdims=True)
            if has_sink:
                denom = denom + jnp.exp(sink - m)
            out = _dot(probs[0].astype(BF16), vals[0])
            for p, v in zip(probs[1:], vals[1:]):
                out = out + _dot(p.astype(BF16), v)
            o_ref[:, qsl] = (out / denom).astype(o_ref.dtype)


def _attn_call(qkv, kvx, *, n_seq, seq, tq, n_q_heads, group, heads, q_col, k_col, v_col,
               kx_col, vx_col, x_row_off, x_len, q_row_off=0, band=True, bias=None,
               bias_per_head=False, sink=None, head_major=False):
    nb = seq // tq
    n_kv_heads = n_q_heads // group
    n_hblk = n_kv_heads // heads
    qw, kw = heads * group * HEAD_DIM, heads * HEAD_DIM
    n_band = 3 if band else 0
    if head_major:
        grid = (nb, n_hblk, n_seq)
        ids = lambda j, h, b: (b, j, h)
    else:
        grid = (n_seq, nb, n_hblk)
        ids = lambda b, j, h: (b, j, h)

    def wrap(f):
        return lambda *g: f(*ids(*g))

    q_blk0, x_blk0 = q_row_off // tq, x_row_off // x_len
    q_spec = pl.BlockSpec((tq, qw), wrap(lambda b, j, h: (q_blk0 + b * nb + j, q_col // qw + h)))
    in_specs, args = [q_spec], [qkv]

    def band_spec(col, shift):
        return pl.BlockSpec((tq, kw), wrap(
            lambda b, j, h: (q_blk0 + b * nb + jnp.clip(j + shift, 0, nb - 1), col // kw + h)))

    if band:
        for col in (k_col, v_col):
            for shift in (-1, 0, 1):
                in_specs.append(band_spec(col, shift))
                args.append(qkv)
    for col in (kx_col, vx_col):
        in_specs.append(pl.BlockSpec((x_len, kw), wrap(lambda b, j, h, col=col: (x_blk0 + b, col // kw + h))))
        args.append(kvx)
    if bias is not None:
        variant = lambda j: jnp.where(j == 0, 0, jnp.where(j == nb - 1, 2, 1))
        hb = heads if bias_per_head else 1
        in_specs.append(pl.BlockSpec((hb, None, tq, 3 * tq), wrap(
            lambda b, j, h: (h if bias_per_head else 0, variant(j), 0, 0))))
        args.append(bias)
    if sink is not None:
        in_specs.append(pl.BlockSpec(memory_space=pltpu.SMEM))
        args.append(sink)
    if sink is not None and head_major:
        raise ValueError("sink lookup assumes the head block is grid axis 2")
    return pl.pallas_call(
        functools.partial(_attn_body, n_band=n_band, group=group, heads=heads, scale=HEAD_DIM ** -0.5,
                          has_bias=bias is not None, bias_per_head=bias_per_head, has_sink=sink is not None),
        grid=grid,
        in_specs=in_specs,
        out_specs=pl.BlockSpec((tq, qw), wrap(lambda b, j, h: (b * nb + j, h))),
        out_shape=jax.ShapeDtypeStruct((n_seq * seq, n_q_heads * HEAD_DIM), BF16),
        compiler_params=_params(),
        name="attn",
    )(*args)


def _swa_mask_table(seq):
    nb = seq // Q_BLOCK
    tables = []
    for j in (0, 1, nb - 1):
        qpos = j * Q_BLOCK + np.arange(Q_BLOCK)
        kpos = (j - 1) * Q_BLOCK + np.arange(3 * Q_BLOCK)
        valid = (np.abs(qpos[:, None] - kpos[None, :]) <= WINDOW) & ((kpos >= 0) & (kpos < seq))[None, :]
        tables.append(np.where(valid, 0.0, NEG_INF).astype(np.float32))
    return jnp.asarray(np.stack(tables)[None])


def _na_bias_table(rel_bias, seq):
    rows = seq // GRID_W
    kh = min(NA_KH_MAX, rows)
    rb_rows = NA_ROWS_PER_BLOCK
    nrb = rows // rb_rows
    rl, c = np.divmod(np.arange(rb_rows * GRID_W), GRID_W)
    kidx = np.arange(3 * rb_rows * GRID_W)
    kb, krem = np.divmod(kidx, rb_rows * GRID_W)
    krl, kc = np.divmod(krem, GRID_W)
    dr_all, valid_all = [], []
    for rb in (0, 1, nrb - 1):
        r = rb_rows * rb + rl
        kr = rb_rows * (rb - 1 + kb) + krl
        r0 = np.clip(r - kh // 2, 0, rows - kh)
        c0 = np.clip(c - NA_KW // 2, 0, GRID_W - NA_KW)
        valid = ((kr[None, :] >= r0[:, None]) & (kr[None, :] < r0[:, None] + kh)
                 & (kc[None, :] >= c0[:, None]) & (kc[None, :] < c0[:, None] + NA_KW)
                 & (kr[None, :] >= 0) & (kr[None, :] < rows))
        dr_all.append(np.clip(kr[None, :] - r[:, None] + NA_KH_MAX - 1, 0, 2 * NA_KH_MAX - 2))
        valid_all.append(valid)
    dc = np.clip(kc[None, :] - c[:, None] + NA_KW - 1, 0, 2 * NA_KW - 2)
    dr = np.stack(dr_all)
    valid = np.stack(valid_all)
    tbl = rel_bias.astype(F32)[:, dr, np.broadcast_to(dc, dr.shape)]
    return jnp.where(jnp.asarray(valid)[None], tbl, NEG_INF)


def _rope_tables(seq, n_lat_rows, n_rows):
    half = HEAD_DIM // 4
    inv_freq = ROPE_BASE ** (-jnp.arange(half, dtype=F32) / half)
    t = jnp.arange(n_lat_rows) % seq
    ang_r = (t // GRID_W).astype(F32)[:, None] * inv_freq[None, :]
    ang_c = (t % GRID_W).astype(F32)[:, None] * inv_freq[None, :]
    cos = jnp.concatenate([jnp.cos(ang_r)] * 2 + [jnp.cos(ang_c)] * 2, axis=-1)
    sin = jnp.concatenate([-jnp.sin(ang_r), jnp.sin(ang_r), -jnp.sin(ang_c), jnp.sin(ang_c)], axis=-1)
    pad = n_rows - n_lat_rows
    cos = jnp.concatenate([cos, jnp.ones((pad, HEAD_DIM), F32)])
    sin = jnp.concatenate([sin, jnp.zeros((pad, HEAD_DIM), F32)])
    return cos, sin


def _expert_body(idx_ref, pos_ref, h_hbm, gate_ref, wg_ref, wu_ref, wd_ref, ys_hbm,
                 xs, ybuf, gsem, ssem, *, cap):
    def gather(c):
        return pltpu.make_async_copy(h_hbm.at[pl.ds(idx_ref[0, c], 1), :], xs.at[pl.ds(c, 1), :], gsem)

    def scatter(c):
        return pltpu.make_async_copy(ybuf.at[pl.ds(c, 1), :], ys_hbm.at[pl.ds(pos_ref[0, c], 1), :], ssem)

    def each(fn):
        def body(c, carry):
            fn(c)
            return carry
        lax.fori_loop(0, cap, body, 0)

    each(lambda c: gather(c).start())
    each(lambda c: gather(c).wait())
    x = xs[...].astype(BF16)
    hg = _dot(x, wg_ref[...])
    hu = _dot(x, wu_ref[...])
    hid = (hg * jax.nn.sigmoid(hg)) * hu
    ybuf[...] = _dot(hid.astype(BF16), wd_ref[...]) * gate_ref[...]
    each(lambda c: scatter(c).start())
    each(lambda c: scatter(c).wait())


def _expert_call(h2, idx_rows, pos_rows, gate, wg, wu, wd, n_sets, cap):
    n_exp, d, ff = wg.shape
    smem_spec = pl.BlockSpec((None, 1, cap), lambda e, b: (b * n_exp + e, 0, 0), memory_space=pltpu.SMEM)
    return pl.pallas_call(
        functools.partial(_expert_body, cap=cap),
        grid=(n_exp, n_sets),
        in_specs=[smem_spec, smem_spec,
                  pl.BlockSpec(memory_space=pl.ANY),
                  pl.BlockSpec((None, cap, 1), lambda e, b: (b * n_exp + e, 0, 0)),
                  pl.BlockSpec((None, d, ff), lambda e, b: (e, 0, 0)),
                  pl.BlockSpec((None, d, ff), lambda e, b: (e, 0, 0)),
                  pl.BlockSpec((None, ff, d), lambda e, b: (e, 0, 0))],
        out_specs=pl.BlockSpec(memory_space=pl.ANY),
        out_shape=jax.ShapeDtypeStruct((n_sets * n_exp * cap, d), F32),
        scratch_shapes=[pltpu.VMEM((cap, d), F32), pltpu.VMEM((cap, d), F32),
                        pltpu.SemaphoreType.DMA, pltpu.SemaphoreType.DMA],
        compiler_params=_params(),
        name="moe_expert",
    )(idx_rows, pos_rows, h2, gate, wg, wu, wd)


_ITEM_FIRST, _ITEM_LAST, _ITEM_VALID = 1, 2, 4


def _combine_body(tb_ref, chunk_ref, flag_ref, ys_ref, tok_ref, res_ref, mod_ref, o_ref, acc_ref, *, tokens):
    b, t = pl.program_id(0), pl.program_id(1)
    flag = flag_ref[b, t]

    @pl.when((flag & _ITEM_FIRST) != 0)
    def _():
        acc_ref[...] = jnp.zeros_like(acc_ref)

    @pl.when((flag & _ITEM_VALID) != 0)
    def _():
        pairs = tok_ref.shape[-1]
        tok = tb_ref[b, t] * tokens + lax.broadcasted_iota(jnp.int32, (tokens, pairs), 0)
        sel = (tok_ref[...] == tok).astype(BF16)
        y = ys_ref[...]
        y_hi = y.astype(BF16)
        y_lo = (y - y_hi.astype(F32)).astype(BF16)
        acc_ref[...] += _dot(sel, y_hi) + _dot(sel, y_lo)

    @pl.when((flag & _ITEM_LAST) != 0)
    def _():
        o_ref[...] = res_ref[...] + mod_ref[...] * acc_ref[...]


def _combine_call(ys, tok_sorted, items, res, mods5, layer, k_gate, *, n_sets, set_len, row_off, t):
    d = res.shape[1]
    tokens = t.moe_tokens
    n_pairs = ys.shape[0] // n_sets
    n_chunks = n_pairs // tokens
    n_blk = set_len // tokens
    item_tb, item_chunk, item_flag = items
    n_items = item_tb.shape[1]
    blk0 = row_off // tokens
    mod_b = (lambda b: b) if row_off == 0 else (lambda b: n_sets)
    grid_spec = pltpu.PrefetchScalarGridSpec(
        num_scalar_prefetch=3,
        grid=(n_sets, n_items),
        in_specs=[
            pl.BlockSpec((tokens, d), lambda b, i, tb, ch, fl: (b * n_chunks + ch[b, i], 0)),
            pl.BlockSpec((None, 1, tokens), lambda b, i, tb, ch, fl: (b * n_chunks + ch[b, i], 0, 0)),
            pl.BlockSpec((tokens, d), lambda b, i, tb, ch, fl: (blk0 + b * n_blk + tb[b, i], 0)),
            pl.BlockSpec((None, None, None, 1, d), lambda b, i, tb, ch, fl: (layer, mod_b(b), k_gate, 0, 0)),
        ],
        out_specs=pl.BlockSpec((tokens, d), lambda b, i, tb, ch, fl: (blk0 + b * n_blk + tb[b, i], 0)),
        scratch_shapes=[pltpu.VMEM((tokens, d), F32)],
    )
    return pl.pallas_call(
        functools.partial(_combine_body, tokens=tokens),
        grid_spec=grid_spec,
        out_shape=jax.ShapeDtypeStruct(res.shape, F32),
        input_output_aliases={5: 0},
        compiler_params=_params(),
        name="moe_combine",
    )(item_tb, item_chunk, item_flag, ys, tok_sorted.reshape(n_sets * n_chunks, 1, tokens), res, mods5)


def _route(aff, n_sets, set_len, n_exp, row_off, tokens):
    cap = EC_CAPACITY_FACTOR * set_len // n_exp
    n_pairs = n_exp * cap
    aff = aff[row_off:row_off + n_sets * set_len, :n_exp].reshape(n_sets, set_len, n_exp)
    gate, idx = lax.top_k(aff.transpose(0, 2, 1), cap)
    flat = idx.reshape(n_sets, n_pairs)
    order = jnp.argsort(flat, axis=1)
    tok_sorted = jnp.take_along_axis(flat, order, axis=1)
    pos = jnp.argsort(order, axis=1)
    set_id = jnp.arange(n_sets, dtype=jnp.int32)[:, None]
    idx_rows = (row_off + set_id * set_len + flat).reshape(n_sets * n_exp, 1, cap).astype(jnp.int32)
    pos_rows = (set_id * n_pairs + pos).reshape(n_sets * n_exp, 1, cap).astype(jnp.int32)

    n_blk, n_chunks = set_len // tokens, n_pairs // tokens
    bounds = jnp.arange(n_blk + 1, dtype=jnp.int32) * tokens
    start = jnp.sum(tok_sorted[:, :, None] < bounds[None, None, :], axis=1).astype(jnp.int32)
    lo, hi = start[:, :-1], start[:, 1:]
    first_chunk = jnp.minimum(lo // tokens, n_chunks - 1)
    last_chunk = jnp.where(hi > lo, (hi - 1) // tokens, first_chunk)
    count = last_chunk - first_chunk + 1
    end = jnp.cumsum(count, axis=1)
    begin = end - count
    n_items = n_chunks + n_blk - 1
    item = jnp.arange(n_items, dtype=jnp.int32)
    tb = jnp.minimum(jnp.sum(end[:, None, :] <= item[None, :, None], axis=2), n_blk - 1).astype(jnp.int32)
    pick = lambda a: jnp.take_along_axis(a, tb, axis=1)
    valid = item[None, :] < end[:, -1:]
    chunk = jnp.minimum(pick(first_chunk) + item[None, :] - pick(begin), pick(last_chunk))
    is_first = valid & (item[None, :] == pick(begin))
    is_last = valid & (item[None, :] == pick(end) - 1)
    flag = (is_first * _ITEM_FIRST + is_last * _ITEM_LAST + valid * _ITEM_VALID).astype(jnp.int32)
    items = (tb, chunk.astype(jnp.int32), flag)
    return idx_rows, pos_rows, gate.reshape(n_sets * n_exp, cap, 1), tok_sorted.astype(jnp.int32), items, cap


def _moe(xa, h2, aff, streams, moe_w, mods5, layer, t):
    wg, wu, wd = moe_w
    n_exp = wg.shape[0]
    for n_sets, set_len, row_off in streams:
        idx_rows, pos_rows, gate, tok_sorted, items, cap = _route(aff, n_sets, set_len, n_exp, row_off, t.moe_tokens)
        ys = _expert_call(h2, idx_rows, pos_rows, gate, wg, wu, wd, n_sets, cap)
        xa = _combine_call(ys, tok_sorted, items, xa, mods5, layer, 5, n_sets=n_sets, set_len=set_len,
                           row_off=row_off, t=t)
    return xa


def _forward(x, c, ctx, c_ctx, norm1_gain, norm2_gain, final_gain, w_ada, b_ada,
             pool_w, pool_scale, conv_w_in, conv_w, conv_w_out,
             swa_w_qkv, swa_w_o, swa_sink, na_w_qkv, na_w_o, na_rel_bias,
             moe_router, moe_w_gate, moe_w_up, moe_w_down, t=Tiles()):
    bsz, seq, d = x.shape
    n_ctx = ctx.shape[1]
    depth = w_ada.shape[0]
    n_exp = moe_router.shape[-1]
    n_lat, n_all = bsz * seq, bsz * (seq + n_ctx)
    n_heads = d // HEAD_DIM
    n_kv = n_heads // KV_GROUP
    kv_dim = n_kv * HEAD_DIM
    assert bsz + 1 <= MOD_ROWS and seq % t.mm_rows == 0 and n_all % t.mm_rows == 0

    xa = jnp.concatenate([x.reshape(n_lat, d), ctx.reshape(bsz * n_ctx, d)], axis=0)
    cc = jnp.concatenate([c, c_ctx[None], jnp.zeros((MOD_ROWS - bsz - 1, d), F32)], axis=0)
    mods5 = _ada_call(cc, w_ada, b_ada, t).reshape(depth, MOD_ROWS, N_MOD, 1, d)
    w_router = jnp.pad(moe_router, ((0, 0), (0, 0), (0, LANE - n_exp))).astype(BF16)
    lat_stream, ctx_stream = (bsz, seq, 0), (bsz, n_ctx, n_lat)
    lat = (seq, bsz)

    for i in range(depth):
        m, j = i % 4, i // 4
        need_ctx = i < depth - 1
        rows_out = n_all if need_ctx else n_lat
        streams = [lat_stream, ctx_stream] if need_ctx else [lat_stream]
        rows_in = n_all if (need_ctx or m >= 2) else n_lat
        h = _norm_call(xa, norm1_gain[i], mods5, i, 0, 1, lat, t, n_rows=rows_in,
                       out_dtype=F32 if m == 0 else BF16)
        if m == 0:
            pooled = [_pool_call(h, s_len, n_s, off, t) for n_s, s_len, off in streams]
            y_in = jnp.concatenate(pooled, axis=0) if len(pooled) > 1 else pooled[0]
            xa = _mm_res(y_in, pool_w[j].astype(BF16), xa, mods5, i, 2, lat, t, n_rows=rows_out,
                         col_scale=pool_scale[j], grouped=True)
        elif m == 1:
            bg, u = _mm_conv_in(h, conv_w_in[j].astype(BF16), t)
            gated = [_conv_gate_call(bg, u, conv_w[j], s_len, n_s, off, t) for n_s, s_len, off in streams]
            y_in = jnp.concatenate(gated, axis=0) if len(gated) > 1 else gated[0]
            xa = _mm_res(y_in, conv_w_out[j].astype(BF16), xa, mods5, i, 2, lat, t, n_rows=rows_out)
        elif m == 2:
            cos, sin = _rope_tables(seq, n_lat, n_all)
            qkv = _mm_rope(h, swa_w_qkv[j].astype(BF16), cos, sin, d + kv_dim, t)
            common = dict(n_seq=bsz, n_q_heads=n_heads, group=KV_GROUP, heads=t.swa_kv_heads,
                          q_col=0, k_col=d, v_col=d + kv_dim, kx_col=d, vx_col=d + kv_dim,
                          x_row_off=n_lat, x_len=n_ctx, sink=swa_sink[j])
            outs = [_attn_call(qkv, qkv, seq=seq, tq=Q_BLOCK, bias=_swa_mask_table(seq), **common)]
            if need_ctx:
                outs.append(_attn_call(qkv, qkv, seq=n_ctx, tq=n_ctx, q_row_off=n_lat, band=False, **common))
            y_in = jnp.concatenate(outs, axis=0) if len(outs) > 1 else outs[0]
            xa = _mm_res(y_in, swa_w_o[j].astype(BF16), xa, mods5, i, 2, lat, t, n_rows=rows_out)
        else:
            qkv = _mm_plain(h, na_w_qkv[j].astype(BF16), t)
            tq = NA_ROWS_PER_BLOCK * GRID_W
            common = dict(n_seq=bsz, n_q_heads=n_heads, group=1, heads=t.na_heads,
                          q_col=0, k_col=d, v_col=2 * d, kx_col=d, vx_col=2 * d,
                          x_row_off=n_lat, x_len=n_ctx)
            outs = [_attn_call(qkv, qkv, seq=seq, tq=tq, bias=_na_bias_table(na_rel_bias[j], seq),
                               bias_per_head=True, head_major=True, **common)]
            if need_ctx:
                outs.append(_attn_call(qkv, qkv, seq=n_ctx, tq=n_ctx, q_row_off=n_lat, band=False, **common))
            y_in = jnp.concatenate(outs, axis=0) if len(outs) > 1 else outs[0]
            xa = _mm_res(y_in, na_w_o[j].astype(BF16), xa, mods5, i, 2, lat, t, n_rows=rows_out)

        h2, aff = _norm_call(xa, norm2_gain[i], mods5, i, 3, 4, lat, t, n_rows=rows_out, out_dtype=F32,
                             w_router=w_router[i], n_experts=n_exp)
        moe_w = (moe_w_gate[i].astype(BF16), moe_w_up[i].astype(BF16), moe_w_down[i].astype(BF16))
        xa = _moe(xa, h2, aff, streams, moe_w, mods5, i, t)

    out = _norm_call(xa, final_gain, None, 0, 0, 0, lat, t, n_rows=n_lat, out_dtype=F32, modulate=False)
    return out.reshape(bsz, seq, d)


def kernel(x, c, ctx, c_ctx, norm1_gain, norm2_gain, final_gain, w_ada, b_ada, pool_w, pool_scale, conv_w_in, conv_w, conv_w_out, swa_w_qkv, swa_w_o, swa_sink, na_w_qkv, na_w_o, na_rel_bias, moe_router, moe_w_gate, moe_w_up, moe_w_down):
    return _forward(x, c, ctx, c_ctx, norm1_gain, norm2_gain, final_gain, w_ada, b_ada,
                    pool_w, pool_scale, conv_w_in, conv_w, conv_w_out,
                    swa_w_qkv, swa_w_o, swa_sink, na_w_qkv, na_w_o, na_rel_bias,
                    moe_router, moe_w_gate, moe_w_up, moe_w_down)
```

```python
import functools
from typing import NamedTuple

import numpy as np
import jax
import jax.numpy as jnp
from jax import lax
from jax.experimental import pallas as pl
from jax.experimental.pallas import tpu as pltpu

GRID_W = 64
HEAD_DIM = 128
KV_GROUP = 4
N_MOD = 6
EPS = 1e-6
NEG_INF = -1e30
POOL_WINDOWS = (2, 4, 8, 16)
CONV_W = 3
WINDOW = 128
Q_BLOCK = 128
ROPE_BASE = 10000.0
NA_KH_MAX = 8
NA_KW = 16
NA_ROWS_PER_BLOCK = 4
EC_CAPACITY_FACTOR = 2
LANE = 128
MOD_ROWS = 16
V7X_VMEM_LIMIT = 56 * 1024 * 1024

BF16 = jnp.bfloat16
F32 = jnp.float32


class Tiles(NamedTuple):
    norm_rows: int = 256
    mm_rows: int = 1024
    mm_cols: int = 1024
    res_cols: int = 512
    conv_cols: int = 512
    ada_cols: int = 512
    pool_cols: int = 512
    pool_chunk: int = 256
    gate_cols: int = 512
    swa_kv_heads: int = 2
    na_heads: int = 4
    moe_tokens: int = 256


def _params(vmem=V7X_VMEM_LIMIT):
    return pltpu.CompilerParams(vmem_limit_bytes=vmem)


def _dot(a, b):
    return jnp.dot(a, b, preferred_element_type=F32)


def _dot_nt(a, b):
    return lax.dot_general(a, b, (((1,), (1,)), ((), ())), preferred_element_type=F32)


def _ada_body(c_ref, w_ref, b_ref, o_ref):
    cvec = c_ref[...]
    a = (cvec * jax.nn.sigmoid(cvec)).astype(BF16)
    o_ref[...] = _dot(a, w_ref[...].astype(BF16)) + b_ref[...]


def _ada_call(cc, w_ada, b_ada, t):
    depth, d, nd = w_ada.shape
    bn = t.ada_cols
    return pl.pallas_call(
        _ada_body,
        grid=(depth, nd // bn),
        in_specs=[
            pl.BlockSpec((MOD_ROWS, d), lambda l, j: (0, 0)),
            pl.BlockSpec((None, d, bn), lambda l, j: (l, 0, j)),
            pl.BlockSpec((None, 1, bn), lambda l, j: (l, 0, j)),
        ],
        out_specs=pl.BlockSpec((None, MOD_ROWS, bn), lambda l, j: (l, 0, j)),
        out_shape=jax.ShapeDtypeStruct((depth, MOD_ROWS, nd), F32),
        compiler_params=_params(),
        name="adaln",
    )(cc, w_ada, b_ada.reshape(depth, 1, nd))


def _mod_spec(layer, k, cols, lat, bm, col_map=None):
    seq, bsz = lat
    sample = lambda i: jnp.minimum((i * bm) // seq, bsz)
    if col_map is None:
        return pl.BlockSpec((None, None, None, 1, cols), lambda i: (layer, sample(i), k, 0, 0))
    return pl.BlockSpec((None, None, None, 1, cols),
                        lambda j, i: (layer, sample(i), k, 0, col_map(j)))


def _norm_mod(x, g_ref, sh_ref, sc_ref):
    y = x * lax.rsqrt(jnp.mean(x * x, axis=-1, keepdims=True) + EPS)
    y = y * g_ref[...]
    return y * (1 + sc_ref[...]) + sh_ref[...]


def _norm_body(x_ref, g_ref, sh_ref, sc_ref, o_ref):
    o_ref[...] = _norm_mod(x_ref[...], g_ref, sh_ref, sc_ref).astype(o_ref.dtype)


def _router_body(x_ref, g_ref, sh_ref, sc_ref, wr_ref, aff_ref, *, n_experts):
    h2 = _norm_mod(x_ref[...], g_ref, sh_ref, sc_ref)
    logits = _dot(h2.astype(BF16), wr_ref[...])
    lane = lax.broadcasted_iota(jnp.int32, logits.shape, 1)
    logits = jnp.where(lane < n_experts, logits, NEG_INF)
    m = jnp.max(logits, axis=-1, keepdims=True)
    p = jnp.exp(logits - m)
    aff_ref[...] = p / jnp.sum(p, axis=-1, keepdims=True)


def _norm_call(x, gain, mods5, layer, k_shift, k_scale, lat, t, *, n_rows, out_dtype=None, w_router=None,
               n_experts=0):
    rows, d = x.shape
    bm = t.norm_rows
    in_specs = [pl.BlockSpec((bm, d), lambda i: (i, 0)),
                pl.BlockSpec((1, d), lambda i: (0, 0)),
                _mod_spec(layer, k_shift, d, lat, bm), _mod_spec(layer, k_scale, d, lat, bm)]
    args = [x, gain.reshape(1, d), mods5, mods5]
    if w_router is None:
        body, out_cols, name = _norm_body, d, "norm_mod"
    else:
        body, out_cols, out_dtype, name = functools.partial(_router_body, n_experts=n_experts), LANE, F32, "router"
        in_specs.append(pl.BlockSpec((d, LANE), lambda i: (0, 0)))
        args.append(w_router)
    return pl.pallas_call(
        body,
        grid=(n_rows // bm,),
        in_specs=in_specs,
        out_specs=pl.BlockSpec((bm, out_cols), lambda i: (i, 0)),
        out_shape=jax.ShapeDtypeStruct((n_rows, out_cols), out_dtype),
        compiler_params=_params(),
        name=name,
    )(*args)


def _mm_plain_body(x_ref, w_ref, o_ref):
    o_ref[...] = _dot(x_ref[...], w_ref[...]).astype(o_ref.dtype)


def _mm_plain(x, w, t, *, col_off=0, n_cols=None, out_dtype=BF16):
    rows, k = x.shape
    n_cols = w.shape[1] - col_off if n_cols is None else n_cols
    bm, bn = t.mm_rows, t.mm_cols
    off = col_off // bn
    return pl.pallas_call(
        _mm_plain_body,
        grid=(n_cols // bn, rows // bm),
        in_specs=[pl.BlockSpec((bm, k), lambda j, i: (i, 0)),
                  pl.BlockSpec((k, bn), lambda j, i: (0, off + j))],
        out_specs=pl.BlockSpec((bm, bn), lambda j, i: (i, j)),
        out_shape=jax.ShapeDtypeStruct((rows, n_cols), out_dtype),
        compiler_params=_params(),
        name="mm_plain",
    )(x, w)


def _rope_rotate(blk, lane):
    fwd = pltpu.roll(blk, LANE - HEAD_DIM // 4, 1)
    bwd = pltpu.roll(blk, HEAD_DIM // 4, 1)
    return jnp.where(lane % (HEAD_DIM // 2) < HEAD_DIM // 4, fwd, bwd)


def _mm_rope_body(x_ref, w_ref, cos_ref, sin_ref, o_ref, *, n_rope_blocks):
    acc = _dot(x_ref[...], w_ref[...])
    j = pl.program_id(0)

    @pl.when(j < n_rope_blocks)
    def _():
        cos, sin = cos_ref[...], sin_ref[...]
        lane = lax.broadcasted_iota(jnp.int32, cos.shape, 1)
        for hd in range(acc.shape[1] // HEAD_DIM):
            sl = slice(hd * HEAD_DIM, (hd + 1) * HEAD_DIM)
            blk = acc[:, sl]
            o_ref[:, sl] = (blk * cos + _rope_rotate(blk, lane) * sin).astype(o_ref.dtype)

    @pl.when(j >= n_rope_blocks)
    def _():
        o_ref[...] = acc.astype(o_ref.dtype)


def _mm_rope(x, w, cos, sin, n_rope_cols, t):
    rows, k = x.shape
    n = w.shape[1]
    bm, bn = t.mm_rows, t.mm_cols
    return pl.pallas_call(
        functools.partial(_mm_rope_body, n_rope_blocks=n_rope_cols // bn),
        grid=(n // bn, rows // bm),
        in_specs=[pl.BlockSpec((bm, k), lambda j, i: (i, 0)),
                  pl.BlockSpec((k, bn), lambda j, i: (0, j)),
                  pl.BlockSpec((bm, HEAD_DIM), lambda j, i: (i, 0)),
                  pl.BlockSpec((bm, HEAD_DIM), lambda j, i: (i, 0))],
        out_specs=pl.BlockSpec((bm, bn), lambda j, i: (i, j)),
        out_shape=jax.ShapeDtypeStruct((rows, n), BF16),
        compiler_params=_params(),
        name="mm_rope",
    )(x, w, cos, sin)


def _mm_conv_in_body(x_ref, wb_ref, wc_ref, wv_ref, b_ref, u_ref):
    x = x_ref[...]
    b_ref[...] = _dot(x, wb_ref[...]).astype(b_ref.dtype)
    u_ref[...] = (_dot(x, wc_ref[...]) * _dot(x, wv_ref[...])).astype(u_ref.dtype)


def _mm_conv_in(x, w_in, t):
    rows, k = x.shape
    d = w_in.shape[1] // 3
    bm, bn = t.mm_rows, t.conv_cols
    nb = d // bn
    wspec = lambda part: pl.BlockSpec((k, bn), lambda j, i: (0, part * nb + j))
    ospec = pl.BlockSpec((bm, bn), lambda j, i: (i, j))
    return pl.pallas_call(
        _mm_conv_in_body,
        grid=(nb, rows // bm),
        in_specs=[pl.BlockSpec((bm, k), lambda j, i: (i, 0)), wspec(0), wspec(1), wspec(2)],
        out_specs=[ospec, ospec],
        out_shape=[jax.ShapeDtypeStruct((rows, d), BF16)] * 2,
        compiler_params=_params(),
        name="mm_conv_in",
    )(x, w_in, w_in, w_in)


def _mm_res_body(*refs, n_parts, first_blocks, col_scale):
    x_refs = refs[:n_parts]
    w_ref, res_ref, mod_ref = refs[n_parts:n_parts + 3]
    o_ref = refs[-1]

    def emit(x_ref):
        acc = _dot(x_ref[...], w_ref[...])
        if col_scale:
            acc = acc * refs[n_parts + 3][...]
        o_ref[...] = res_ref[...] + mod_ref[...] * acc

    if n_parts == 1:
        emit(x_refs[0])
    else:
        i = pl.program_id(1)
        pl.when(i < first_blocks)(lambda: emit(x_refs[0]))
        pl.when(i >= first_blocks)(lambda: emit(x_refs[1]))


def _mm_res(xs, w, res, mods5, layer, k_gate, lat, t, *, col_scale=None, grouped=False):
    k = xs[0].shape[1]
    d = res.shape[1]
    bm = t.mm_rows
    part_blocks = [x.shape[0] // bm for x in xs]
    first = part_blocks[0]
    row_maps = [lambda i: jnp.minimum(i, first - 1), lambda i: jnp.maximum(i - first, 0)]
    if grouped:
        kg = w.shape[1]
        bn = kg
        x_specs = [pl.BlockSpec((bm, kg), lambda j, i, rm=rm: (rm(i), j)) for rm in row_maps[:len(xs)]]
        w_spec = pl.BlockSpec((None, kg, kg), lambda j, i: (j, 0, 0))
    else:
        bn = t.res_cols
        x_specs = [pl.BlockSpec((bm, k), lambda j, i, rm=rm: (rm(i), 0)) for rm in row_maps[:len(xs)]]
        w_spec = pl.BlockSpec((k, bn), lambda j, i: (0, j))
    in_specs = x_specs + [w_spec,
                          pl.BlockSpec((bm, bn), lambda j, i: (i, j)),
                          _mod_spec(layer, k_gate, bn, lat, bm, col_map=lambda j: j)]
    args = [*xs, w, res, mods5]
    if col_scale is not None:
        in_specs.append(pl.BlockSpec((1, bn), lambda j, i: (0, j)))
        args.append(col_scale.reshape(1, d))
    return pl.pallas_call(
        functools.partial(_mm_res_body, n_parts=len(xs), first_blocks=first, col_scale=col_scale is not None),
        grid=(d // bn, sum(part_blocks)),
        in_specs=in_specs,
        out_specs=pl.BlockSpec((bm, bn), lambda j, i: (i, j)),
        out_shape=jax.ShapeDtypeStruct(res.shape, F32),
        input_output_aliases={len(xs) + 1: 0},
        compiler_params=_params(),
        name="mm_res",
    )(*args)


_POOL_HALO = 16


def _pool_body(h_ref, *rest, seq, chunk, groups_per_col):
    o_ref, pad_ref = rest[-2:]
    cols = h_ref.shape[1]
    zeros = jnp.zeros((_POOL_HALO, cols), F32)
    pad_ref[pl.ds(0, _POOL_HALO), :] = zeros
    pad_ref[pl.ds(_POOL_HALO + seq, _POOL_HALO), :] = zeros
    pad_ref[pl.ds(_POOL_HALO, seq), :] = h_ref[...]
    win_rows = chunk + 2 * _POOL_HALO
    group = pl.program_id(1) // groups_per_col

    for g, w in enumerate(POOL_WINDOWS):
        @pl.when(group == g)
        def _(w=w):
            def body(c, carry):
                start = pl.multiple_of(c * chunk, chunk)
                win = pad_ref[pl.ds(start, win_rows), :]
                s, width = win, 1
                while width < w:
                    s = s + pltpu.roll(s, width, 0)
                    width *= 2
                lead = w // 2 - 1
                if lead:
                    s = pltpu.roll(s, win_rows - lead, 0)
                tok = start + lax.broadcasted_iota(jnp.int32, (chunk, 1), 0)
                cnt = (jnp.minimum(tok + w // 2, seq) - jnp.maximum(tok - w // 2, 0)).astype(F32)
                centre = slice(_POOL_HALO, _POOL_HALO + chunk)
                o_ref[pl.ds(start, chunk), :] = (s[centre] / cnt - win[centre]).astype(o_ref.dtype)
                return carry
            lax.fori_loop(0, seq // chunk, body, 0)


def _pool_call(h, seq, n_seq, row_off, t):
    rows, d = h.shape
    bc = t.pool_cols
    chunk = min(t.pool_chunk, seq)
    group_cols = d // len(POOL_WINDOWS)
    off = row_off // seq
    return pl.pallas_call(
        functools.partial(_pool_body, seq=seq, chunk=chunk, groups_per_col=group_cols // bc),
        grid=(n_seq, d // bc),
        in_specs=[pl.BlockSpec((seq, bc), lambda b, j: (off + b, j))],
        out_specs=pl.BlockSpec((seq, bc), lambda b, j: (b, j)),
        out_shape=jax.ShapeDtypeStruct((n_seq * seq, d), BF16),
        scratch_shapes=[pltpu.VMEM((seq + 2 * _POOL_HALO, bc), F32)],
        compiler_params=_params(),
        name="pool",
    )(h)


def _conv_gate_body(b_ref, u_ref, w_ref, *rest, seq):
    o_ref = rest[-1]
    u = u_ref[...].astype(F32)
    row = lax.broadcasted_iota(jnp.int32, (seq, 1), 0)
    prev = jnp.where(row == 0, 0.0, pltpu.roll(u, 1, 0))
    nxt = jnp.where(row == seq - 1, 0.0, pltpu.roll(u, seq - 1, 0))
    conv = prev * w_ref[0:1, :] + u * w_ref[1:2, :] + nxt * w_ref[2:3, :]
    o_ref[...] = (b_ref[...].astype(F32) * conv).astype(o_ref.dtype)


def _conv_gate_call(bg, u, conv_w, seq, n_seq, row_off, t):
    rows, d = bg.shape
    bc = t.gate_cols
    off = row_off // seq
    spec = lambda: pl.BlockSpec((seq, bc), lambda b, j: (off + b, j))
    return pl.pallas_call(
        functools.partial(_conv_gate_body, seq=seq),
        grid=(n_seq, d // bc),
        in_specs=[spec(), spec(), pl.BlockSpec((CONV_W, bc), lambda b, j: (0, j))],
        out_specs=pl.BlockSpec((seq, bc), lambda b, j: (b, j)),
        out_shape=jax.ShapeDtypeStruct((n_seq * seq, d), BF16),
        compiler_params=_params(),
        name="conv_gate",
    )(bg, u, conv_w)


def _attn_body(*refs, n_band, group, heads, scale, has_bias, bias_per_head, has_sink):
    q_ref = refs[0]
    pos = 1
    k_band = refs[pos:pos + n_band]
    v_band = refs[pos + n_band:pos + 2 * n_band]
    pos += 2 * n_band
    kx_ref, vx_ref = refs[pos], refs[pos + 1]
    pos += 2
    bias_ref = sink_ref = None
    if has_bias:
        bias_ref = refs[pos]
        pos += 1
    if has_sink:
        sink_ref = refs[pos]
        pos += 1
    o_ref = refs[-1]
    kv_block = pl.program_id(2) if has_sink else 0

    for hh in range(heads):
        ksl = slice(hh * HEAD_DIM, (hh + 1) * HEAD_DIM)
        keys = [r[:, ksl] for r in k_band] + [kx_ref[:, ksl]]
        vals = [r[:, ksl] for r in v_band] + [vx_ref[:, ksl]]
        for g in range(group):
            qh = hh * group + g
            qsl = slice(qh * HEAD_DIM, (qh + 1) * HEAD_DIM)
            q = q_ref[:, qsl]
            scores = [_dot_nt(q, k) * scale for k in keys]
            if has_bias:
                tk = k_band[0].shape[0]
                for bi in range(n_band):
                    bsl = slice(bi * tk, (bi + 1) * tk)
                    bias = bias_ref[hh, :, bsl] if bias_per_head else bias_ref[0, :, bsl]
                    scores[bi] = scores[bi] + bias
            m = scores[0].max(axis=-1, keepdims=True)
            for s in scores[1:]:
                m = jnp.maximum(m, s.max(axis=-1, keepdims=True))
            if has_sink:
                sink = sink_ref[kv_block * heads * group + qh]
                m = jnp.maximum(m, sink)
            probs = [jnp.exp(s - m) for s in scores]
            denom = probs[0].sum(axis=-1, keepdims=True)
            for p in probs[1:]:
                denom = denom + p.sum(axis=-1, keepdims=True)
            if has_sink:
                denom = denom + jnp.exp(sink - m)
            out = _dot(probs[0].astype(BF16), vals[0])
            for p, v in zip(probs[1:], vals[1:]):
                out = out + _dot(p.astype(BF16), v)
            o_ref[:, qsl] = (out / denom).astype(o_ref.dtype)


def _attn_call(qkv, kvx, *, n_seq, seq, tq, n_q_heads, group, heads, q_col, k_col, v_col,
               kx_col, vx_col, x_row_off, x_len, q_row_off=0, band=True, bias=None,
               bias_per_head=False, sink=None, head_major=False):
    nb = seq // tq
    n_kv_heads = n_q_heads // group
    n_hblk = n_kv_heads // heads
    qw, kw = heads * group * HEAD_DIM, heads * HEAD_DIM
    n_band = 3 if band else 0
    if head_major:
        grid = (nb, n_hblk, n_seq)
        ids = lambda j, h, b: (b, j, h)
    else:
        grid = (n_seq, nb, n_hblk)
        ids = lambda b, j, h: (b, j, h)

    def wrap(f):
        return lambda *g: f(*ids(*g))

    q_blk0, x_blk0 = q_row_off // tq, x_row_off // x_len
    q_spec = pl.BlockSpec((tq, qw), wrap(lambda b, j, h: (q_blk0 + b * nb + j, q_col // qw + h)))
    in_specs, args = [q_spec], [qkv]

    def band_spec(col, shift):
        return pl.BlockSpec((tq, kw), wrap(
            lambda b, j, h: (q_blk0 + b * nb + jnp.clip(j + shift, 0, nb - 1), col // kw + h)))

    if band:
        for col in (k_col, v_col):
            for shift in (-1, 0, 1):
                in_specs.append(band_spec(col, shift))
                args.append(qkv)
    for col in (kx_col, vx_col):
        in_specs.append(pl.BlockSpec((x_len, kw), wrap(lambda b, j, h, col=col: (x_blk0 + b, col // kw + h))))
        args.append(kvx)
    if bias is not None:
        variant = lambda j: jnp.where(j == 0, 0, jnp.where(j == nb - 1, 2, 1))
        hb = heads if bias_per_head else 1
        in_specs.append(pl.BlockSpec((hb, None, tq, 3 * tq), wrap(
            lambda b, j, h: (h if bias_per_head else 0, variant(j), 0, 0))))
        args.append(bias)
    if sink is not None:
        in_specs.append(pl.BlockSpec(memory_space=pltpu.SMEM))
        args.append(sink)
    if sink is not None and head_major:
        raise ValueError("sink lookup assumes the head block is grid axis 2")
    return pl.pallas_call(
        functools.partial(_attn_body, n_band=n_band, group=group, heads=heads, scale=HEAD_DIM ** -0.5,
                          has_bias=bias is not None, bias_per_head=bias_per_head, has_sink=sink is not None),
        grid=grid,
        in_specs=in_specs,
        out_specs=pl.BlockSpec((tq, qw), wrap(lambda b, j, h: (b * nb + j, h))),
        out_shape=jax.ShapeDtypeStruct((n_seq * seq, n_q_heads * HEAD_DIM), BF16),
        compiler_params=_params(),
        name="attn",
    )(*args)


def _swa_mask_table(seq):
    nb = seq // Q_BLOCK
    tables = []
    for j in (0, 1, nb - 1):
        qpos = j * Q_BLOCK + np.arange(Q_BLOCK)
        kpos = (j - 1) * Q_BLOCK + np.arange(3 * Q_BLOCK)
        valid = (np.abs(qpos[:, None] - kpos[None, :]) <= WINDOW) & ((kpos >= 0) & (kpos < seq))[None, :]
        tables.append(np.where(valid, 0.0, NEG_INF).astype(np.float32))
    return jnp.asarray(np.stack(tables)[None])


def _na_bias_table(rel_bias, seq):
    rows = seq // GRID_W
    kh = min(NA_KH_MAX, rows)
    rb_rows = NA_ROWS_PER_BLOCK
    nrb = rows // rb_rows
    rl, c = np.divmod(np.arange(rb_rows * GRID_W), GRID_W)
    kidx = np.arange(3 * rb_rows * GRID_W)
    kb, krem = np.divmod(kidx, rb_rows * GRID_W)
    krl, kc = np.divmod(krem, GRID_W)
    valid_all = []
    for rb in (0, 1, nrb - 1):
        r = rb_rows * rb + rl
        kr = rb_rows * (rb - 1 + kb) + krl
        r0 = np.clip(r - kh // 2, 0, rows - kh)
        c0 = np.clip(c - NA_KW // 2, 0, GRID_W - NA_KW)
        valid_all.append((kr[None, :] >= r0[:, None]) & (kr[None, :] < r0[:, None] + kh)
                         & (kc[None, :] >= c0[:, None]) & (kc[None, :] < c0[:, None] + NA_KW)
                         & (kr[None, :] >= 0) & (kr[None, :] < rows))
    valid = np.stack(valid_all)
    n_h, n_dr, n_dc = rel_bias.shape
    edge = GRID_W - NA_KW
    padded = jnp.pad(rel_bias.astype(F32), ((0, 0), (0, 0), (edge, edge)), mode="edge")
    toep = jnp.stack([padded[:, :, GRID_W - 1 - ci:2 * GRID_W - 1 - ci] for ci in range(GRID_W)], axis=2)
    dr0 = NA_KH_MAX - 1 - rb_rows
    assert dr0 - (rb_rows - 1) >= 0 and dr0 + 3 * rb_rows - 1 < n_dr
    blocks = jnp.stack([toep[:, dr0 - q:dr0 - q + 3 * rb_rows] for q in range(rb_rows)], axis=1)
    tbl = blocks.transpose(0, 1, 3, 2, 4).reshape(n_h, rb_rows * GRID_W, 3 * rb_rows * GRID_W)
    return jnp.where(jnp.asarray(valid)[None], tbl[:, None], NEG_INF)


def _rope_tables(seq, n_lat_rows, n_rows):
    half = HEAD_DIM // 4
    inv_freq = ROPE_BASE ** (-jnp.arange(half, dtype=F32) / half)
    t = jnp.arange(n_lat_rows) % seq
    ang_r = (t // GRID_W).astype(F32)[:, None] * inv_freq[None, :]
    ang_c = (t % GRID_W).astype(F32)[:, None] * inv_freq[None, :]
    cos = jnp.concatenate([jnp.cos(ang_r)] * 2 + [jnp.cos(ang_c)] * 2, axis=-1)
    sin = jnp.concatenate([-jnp.sin(ang_r), jnp.sin(ang_r), -jnp.sin(ang_c), jnp.sin(ang_c)], axis=-1)
    pad = n_rows - n_lat_rows
    cos = jnp.concatenate([cos, jnp.ones((pad, HEAD_DIM), F32)])
    sin = jnp.concatenate([sin, jnp.zeros((pad, HEAD_DIM), F32)])
    return cos, sin


_DMA_UNROLL = 8
_GATHER_CHUNKS = 8


def _expert_body(idx_ref, idx_next_ref, pos_ref, x_hbm, gate_ref, gain_ref, sh_ref, sc_ref,
                 wg_ref, wu_ref, wd_ref, ys_hbm, xs, ybuf, gsem, ssem, *, rows, n_steps):
    step = pl.program_id(0) * pl.num_programs(1) + pl.program_id(1)
    slot = step % 2

    def start_gather(iref, dst_slot):
        def body(c, carry):
            pltpu.make_async_copy(x_hbm.at[pl.ds(iref[0, c], 1), :], xs.at[dst_slot, pl.ds(c, 1), :],
                                  gsem.at[dst_slot]).start()
            return carry
        lax.fori_loop(0, rows, body, 0, unroll=_DMA_UNROLL)

    def wait_gather(s):
        pltpu.make_async_copy(x_hbm.at[pl.ds(0, rows), :], xs.at[s], gsem.at[s]).wait()

    def wait_scatter(s):
        pltpu.make_async_copy(ybuf.at[s], ys_hbm.at[pl.ds(0, rows), :], ssem.at[s]).wait()

    @pl.when(step == 0)
    def _():
        start_gather(idx_ref, 0)

    wait_gather(slot)
    h2 = _norm_mod(xs[slot], gain_ref, sh_ref, sc_ref).astype(BF16)
    d = h2.shape[1]
    kw, per = d // _GATHER_CHUNKS, rows // _GATHER_CHUNKS
    hg = hu = None
    for kc in range(_GATHER_CHUNKS):
        ksl = slice(kc * kw, (kc + 1) * kw)
        pg, pu = _dot(h2[:, ksl], wg_ref[ksl, :]), _dot(h2[:, ksl], wu_ref[ksl, :])
        hg, hu = (pg, pu) if hg is None else (hg + pg, hu + pu)
        for c in range(kc * per, (kc + 1) * per):
            pltpu.make_async_copy(x_hbm.at[pl.ds(idx_next_ref[0, c], 1), :],
                                  xs.at[1 - slot, pl.ds(c, 1), :], gsem.at[1 - slot]).start()
    hid = (hg * jax.nn.sigmoid(hg)) * hu
    out = _dot(hid.astype(BF16), wd_ref[...]) * gate_ref[...]

    @pl.when(step >= 2)
    def _():
        wait_scatter(slot)

    ybuf[slot] = out

    def scatter_body(c, carry):
        pltpu.make_async_copy(ybuf.at[slot, pl.ds(c, 1), :], ys_hbm.at[pl.ds(pos_ref[0, c], 1), :],
                              ssem.at[slot]).start()
        return carry
    lax.fori_loop(0, rows, scatter_body, 0, unroll=_DMA_UNROLL)

    @pl.when(step == n_steps - 1)
    def _():
        wait_gather(1 - slot)
        wait_scatter(slot)
        if n_steps > 1:
            wait_scatter(1 - slot)


def _expert_call(xa, gain, mods5, layer, idx_rows, pos_rows, gate, wg, wu, wd, n_groups, n_out_rows, *,
                 n_lat_groups):
    n_exp, d, ff = wg.shape
    rows = idx_rows.shape[-1]
    n_steps = n_exp * n_groups
    step_spec = lambda nxt: pl.BlockSpec(
        (None, 1, rows), lambda e, g: (jnp.minimum(e * n_groups + g + nxt, n_steps - 1), 0, 0),
        memory_space=pltpu.SMEM)
    mod_spec = lambda k: pl.BlockSpec((None, None, None, 1, d),
                                      lambda e, g: (layer, jnp.minimum(g, n_lat_groups), k, 0, 0))
    w_spec = lambda shape: pl.BlockSpec((None,) + shape, lambda e, g: (e, 0, 0))
    return pl.pallas_call(
        functools.partial(_expert_body, rows=rows, n_steps=n_steps),
        grid=(n_exp, n_groups),
        in_specs=[step_spec(0), step_spec(1), step_spec(0),
                  pl.BlockSpec(memory_space=pl.ANY),
                  pl.BlockSpec((None, rows, 1), lambda e, g: (e * n_groups + g, 0, 0)),
                  pl.BlockSpec((1, d), lambda e, g: (0, 0)),
                  mod_spec(3), mod_spec(4),
                  w_spec((d, ff)), w_spec((d, ff)), w_spec((ff, d))],
        out_specs=pl.BlockSpec(memory_space=pl.ANY),
        out_shape=jax.ShapeDtypeStruct((n_out_rows, d), F32),
        scratch_shapes=[pltpu.VMEM((2, rows, d), F32), pltpu.VMEM((2, rows, d), F32),
                        pltpu.SemaphoreType.DMA((2,)), pltpu.SemaphoreType.DMA((2,))],
        compiler_params=_params(),
        name="moe_expert",
    )(idx_rows, idx_rows, pos_rows, xa, gate, gain.reshape(1, d), mods5, mods5, wg, wu, wd)


_ITEM_FIRST, _ITEM_LAST, _ITEM_VALID = 1, 2, 4


def _combine_body(blk_ref, chunk_ref, flag_ref, tb_ref, modrow_ref, ys_ref, tok_ref, res_ref, mod_ref, *rest,
                  tokens, post):
    acc_ref = rest[-1]
    if post is None:
        o_ref = rest[-2]
    else:
        gain_ref, o_ref, h_ref = rest[0], rest[-3], rest[-2]
        if post == "norm_mod":
            sh_ref, sc_ref = rest[1], rest[2]
    b, t = pl.program_id(0), pl.program_id(1)
    flag = flag_ref[b, t]

    @pl.when((flag & _ITEM_FIRST) != 0)
    def _():
        acc_ref[...] = jnp.zeros_like(acc_ref)

    @pl.when((flag & _ITEM_VALID) != 0)
    def _():
        pairs = tok_ref.shape[-1]
        tok = tb_ref[b, t] * tokens + lax.broadcasted_iota(jnp.int32, (tokens, pairs), 0)
        sel = (tok_ref[...] == tok).astype(BF16)
        y = ys_ref[...]
        y_hi = y.astype(BF16)
        y_lo = (y - y_hi.astype(F32)).astype(BF16)
        acc_ref[...] += _dot(sel, y_hi) + _dot(sel, y_lo)

    @pl.when((flag & _ITEM_LAST) != 0)
    def _():
        xn = res_ref[...] + mod_ref[...] * acc_ref[...]
        o_ref[...] = xn
        if post is not None:
            y = xn * lax.rsqrt(jnp.mean(xn * xn, axis=-1, keepdims=True) + EPS)
            y = y * gain_ref[...]
            if post == "norm_mod":
                y = y * (1 + sc_ref[...]) + sh_ref[...]
            h_ref[...] = y.astype(h_ref.dtype)


class _Post(NamedTuple):
    gain: jax.Array
    layer: int | None
    dtype: object
    n_rows: int


def _combine_call(ys, tok_sorted, items, res, mods5, layer, k_gate, t, post):
    d = res.shape[1]
    tokens = t.moe_tokens
    n_sets, n_items = items[0].shape
    n_prefetch = len(items)
    mod_spec = lambda lyr, k: pl.BlockSpec((None, None, None, 1, d),
                                           lambda s, i, blk, ch, fl, tb, mr: (lyr, mr[s], k, 0, 0))
    row_spec = lambda: pl.BlockSpec((tokens, d), lambda s, i, blk, ch, fl, tb, mr: (blk[s, i], 0))
    in_specs = [
        pl.BlockSpec((tokens, d), lambda s, i, blk, ch, fl, tb, mr: (ch[s, i], 0)),
        pl.BlockSpec((None, 1, tokens), lambda s, i, blk, ch, fl, tb, mr: (ch[s, i], 0, 0)),
        row_spec(), mod_spec(layer, k_gate),
        pl.BlockSpec((1, d), lambda s, i, blk, ch, fl, tb, mr: (0, 0))]
    args = [*items, ys, tok_sorted, res, mods5, post.gain.reshape(1, d)]
    mode = "norm" if post.layer is None else "norm_mod"
    if mode == "norm_mod":
        in_specs += [mod_spec(post.layer, 0), mod_spec(post.layer, 1)]
        args += [mods5, mods5]
    return pl.pallas_call(
        functools.partial(_combine_body, tokens=tokens, post=mode),
        grid_spec=pltpu.PrefetchScalarGridSpec(
            num_scalar_prefetch=n_prefetch, grid=(n_sets, n_items), in_specs=in_specs,
            out_specs=[row_spec(), row_spec()],
            scratch_shapes=[pltpu.VMEM((tokens, d), F32)]),
        out_shape=[jax.ShapeDtypeStruct(res.shape, F32), jax.ShapeDtypeStruct((post.n_rows, d), post.dtype)],
        input_output_aliases={n_prefetch + 2: 0},
        compiler_params=_params(),
        name="moe_combine",
    )(*args)


def _route(aff, n_sets, set_len, n_exp, row_off, pair_off, tokens):
    cap = EC_CAPACITY_FACTOR * set_len // n_exp
    n_pairs = n_exp * cap
    aff = aff[row_off:row_off + n_sets * set_len, :n_exp].reshape(n_sets, set_len, n_exp)
    gate, idx = lax.top_k(aff.transpose(0, 2, 1), cap)
    flat = idx.reshape(n_sets, n_pairs)
    order = jnp.argsort(flat, axis=1)
    tok_sorted = jnp.take_along_axis(flat, order, axis=1)
    pos = jnp.argsort(order, axis=1)
    set_id = jnp.arange(n_sets, dtype=jnp.int32)[:, None]
    idx_rows = (row_off + set_id * set_len + flat).reshape(n_sets, n_exp, cap).astype(jnp.int32)
    pos_rows = (pair_off + set_id * n_pairs + pos).reshape(n_sets, n_exp, cap).astype(jnp.int32)

    n_blk, n_chunks = set_len // tokens, n_pairs // tokens
    bounds = jnp.arange(n_blk + 1, dtype=jnp.int32) * tokens
    start = jnp.sum(tok_sorted[:, :, None] < bounds[None, None, :], axis=1).astype(jnp.int32)
    lo, hi = start[:, :-1], start[:, 1:]
    first_chunk = jnp.minimum(lo // tokens, n_chunks - 1)
    last_chunk = jnp.where(hi > lo, (hi - 1) // tokens, first_chunk)
    count = last_chunk - first_chunk + 1
    end = jnp.cumsum(count, axis=1)
    begin = end - count
    n_items = n_chunks + n_blk - 1
    item = jnp.arange(n_items, dtype=jnp.int32)
    tb = jnp.minimum(jnp.sum(end[:, None, :] <= item[None, :, None], axis=2), n_blk - 1).astype(jnp.int32)
    pick = lambda a: jnp.take_along_axis(a, tb, axis=1)
    valid = item[None, :] < end[:, -1:]
    chunk = jnp.minimum(pick(first_chunk) + item[None, :] - pick(begin), pick(last_chunk))
    is_first = valid & (item[None, :] == pick(begin))
    is_last = valid & (item[None, :] == pick(end) - 1)
    flag = (is_first * _ITEM_FIRST + is_last * _ITEM_LAST + valid * _ITEM_VALID).astype(jnp.int32)
    set_col = set_id.astype(jnp.int32)
    blk = row_off // tokens + set_col * n_blk + tb
    chunk = pair_off // tokens + set_col * n_chunks + chunk.astype(jnp.int32)
    items = (blk, chunk, flag, tb)
    return (idx_rows, pos_rows, gate), tok_sorted.astype(jnp.int32).reshape(n_sets * n_chunks, 1, tokens), items


def _expert_groups(per_stream, rows, n_real_pairs):
    cols = ([], [], [])
    n_spare = 0
    for idx, pos, gate in per_stream:
        n_sets, n_exp, cap = idx.shape
        per_exp = n_sets * cap
        pad = -per_exp % rows
        spare = n_real_pairs + n_spare + (jnp.arange(n_exp, dtype=jnp.int32)[:, None] * pad
                                          + jnp.arange(pad, dtype=jnp.int32)[None, :])
        n_spare += n_exp * pad
        fills = (jnp.broadcast_to(idx[0, 0, 0], (n_exp, pad)), spare, jnp.zeros((n_exp, pad), gate.dtype))
        for col, arr, fill in zip(cols, (idx, pos, gate), fills):
            flat = arr.transpose(1, 0, 2).reshape(n_exp, per_exp)
            col.append(jnp.concatenate([flat, fill], axis=1).reshape(n_exp, -1, rows))
    idx, pos, gate = (jnp.concatenate(col, axis=1) for col in cols)
    n_groups = idx.shape[1]
    return (idx.reshape(-1, 1, rows), pos.reshape(-1, 1, rows), gate.reshape(-1, rows, 1), n_groups,
            n_real_pairs + n_spare)


def _moe(xa, aff, streams, gain2, moe_w, mods5, layer, t, post):
    wg, wu, wd = moe_w
    n_exp = wg.shape[0]
    n_lat_sets = streams[0][0]
    routed, pair_off = [], 0
    for n_sets, set_len, row_off in streams:
        routed.append(_route(aff, n_sets, set_len, n_exp, row_off, pair_off, t.moe_tokens))
        pair_off += n_sets * n_exp * (EC_CAPACITY_FACTOR * set_len // n_exp)
    rows = EC_CAPACITY_FACTOR * streams[0][1] // n_exp
    idx_rows, pos_rows, gate, n_groups, n_out = _expert_groups([r[0] for r in routed], rows, pair_off)
    ys = _expert_call(xa, gain2, mods5, layer, idx_rows, pos_rows, gate, wg, wu, wd, n_groups, n_out,
                      n_lat_groups=n_lat_sets)
    n_items = max(r[2][0].shape[1] for r in routed)

    def pad_items(arr, is_flag):
        extra = n_items - arr.shape[1]
        tail = jnp.zeros((arr.shape[0], extra), arr.dtype) if is_flag else jnp.repeat(arr[:, -1:], extra, axis=1)
        return jnp.concatenate([arr, tail], axis=1)

    items = [jnp.concatenate([pad_items(r[2][k], k == 2) for r in routed], axis=0) for k in range(4)]
    mod_row = jnp.concatenate([jnp.arange(n_sets, dtype=jnp.int32) if si == 0
                               else jnp.full((n_sets,), n_lat_sets, jnp.int32)
                               for si, (n_sets, _, _) in enumerate(streams)])
    tok_sorted = jnp.concatenate([r[1] for r in routed], axis=0)
    return _combine_call(ys, tok_sorted, (*items, mod_row), xa, mods5, layer, 5, t, post)


def _forward(x, c, ctx, c_ctx, norm1_gain, norm2_gain, final_gain, w_ada, b_ada,
             pool_w, pool_scale, conv_w_in, conv_w, conv_w_out,
             swa_w_qkv, swa_w_o, swa_sink, na_w_qkv, na_w_o, na_rel_bias,
             moe_router, moe_w_gate, moe_w_up, moe_w_down, t=Tiles()):
    bsz, seq, d = x.shape
    n_ctx = ctx.shape[1]
    depth = w_ada.shape[0]
    n_exp = moe_router.shape[-1]
    n_lat, n_all = bsz * seq, bsz * (seq + n_ctx)
    n_heads = d // HEAD_DIM
    n_kv = n_heads // KV_GROUP
    kv_dim = n_kv * HEAD_DIM
    assert bsz + 1 <= MOD_ROWS and seq % t.mm_rows == 0 and n_all % t.mm_rows == 0

    xa = jnp.concatenate([x.reshape(n_lat, d), ctx.reshape(bsz * n_ctx, d)], axis=0)
    cc = jnp.concatenate([c, c_ctx[None], jnp.zeros((MOD_ROWS - bsz - 1, d), F32)], axis=0)
    mods5 = _ada_call(cc, w_ada, b_ada, t).reshape(depth, MOD_ROWS, N_MOD, 1, d)
    w_router = jnp.pad(moe_router, ((0, 0), (0, 0), (0, LANE - n_exp))).astype(BF16)
    lat_stream, ctx_stream = (bsz, seq, 0), (bsz, n_ctx, n_lat)
    lat = (seq, bsz)

    h_dtype = lambda layer: F32 if layer % 4 == 0 else BF16
    h = _norm_call(xa, norm1_gain[0], mods5, 0, 0, 1, lat, t, n_rows=n_all if depth > 1 else n_lat,
                   out_dtype=h_dtype(0))
    for i in range(depth):
        m, j = i % 4, i // 4
        need_ctx = i < depth - 1
        rows_out = n_all if need_ctx else n_lat
        streams = [lat_stream, ctx_stream] if need_ctx else [lat_stream]

        if m == 0:
            y_in = [_pool_call(h, s_len, n_s, off, t) for n_s, s_len, off in streams]
            xa = _mm_res(y_in, pool_w[j].astype(BF16), xa, mods5, i, 2, lat, t,
                         col_scale=pool_scale[j], grouped=True)
        elif m == 1:
            bg, u = _mm_conv_in(h, conv_w_in[j].astype(BF16), t)
            y_in = [_conv_gate_call(bg, u, conv_w[j], s_len, n_s, off, t) for n_s, s_len, off in streams]
            xa = _mm_res(y_in, conv_w_out[j].astype(BF16), xa, mods5, i, 2, lat, t)
        else:
            if m == 2:
                cos, sin = _rope_tables(seq, n_lat, n_all)
                qkv = _mm_rope(h, swa_w_qkv[j].astype(BF16), cos, sin, d + kv_dim, t)
                common = dict(n_seq=bsz, n_q_heads=n_heads, group=KV_GROUP, heads=t.swa_kv_heads,
                              q_col=0, k_col=d, v_col=d + kv_dim, kx_col=d, vx_col=d + kv_dim,
                              x_row_off=n_lat, x_len=n_ctx, sink=swa_sink[j])
                lat_kw = dict(tq=Q_BLOCK, bias=_swa_mask_table(seq))
                w_o = swa_w_o[j]
            else:
                qkv = _mm_plain(h, na_w_qkv[j].astype(BF16), t)
                common = dict(n_seq=bsz, n_q_heads=n_heads, group=1, heads=t.na_heads,
                              q_col=0, k_col=d, v_col=2 * d, kx_col=d, vx_col=2 * d,
                              x_row_off=n_lat, x_len=n_ctx)
                lat_kw = dict(tq=NA_ROWS_PER_BLOCK * GRID_W, bias=_na_bias_table(na_rel_bias[j], seq),
                              bias_per_head=True, head_major=True)
                w_o = na_w_o[j]
            y_in = [_attn_call(qkv, qkv, seq=seq, **lat_kw, **common)]
            if need_ctx:
                y_in.append(_attn_call(qkv, qkv, seq=n_ctx, tq=n_ctx, q_row_off=n_lat, band=False, **common))
            xa = _mm_res(y_in, w_o.astype(BF16), xa, mods5, i, 2, lat, t)

        aff = _norm_call(xa, norm2_gain[i], mods5, i, 3, 4, lat, t, n_rows=rows_out,
                         w_router=w_router[i], n_experts=n_exp)
        moe_w = (moe_w_gate[i].astype(BF16), moe_w_up[i].astype(BF16), moe_w_down[i].astype(BF16))
        post = (_Post(norm1_gain[i + 1], i + 1, h_dtype(i + 1), n_all) if need_ctx
                else _Post(final_gain, None, F32, n_lat))
        xa, h = _moe(xa, aff, streams, norm2_gain[i], moe_w, mods5, i, t, post)

    return h.reshape(bsz, seq, d)


def kernel(x, c, ctx, c_ctx, norm1_gain, norm2_gain, final_gain, w_ada, b_ada, pool_w, pool_scale, conv_w_in, conv_w, conv_w_out, swa_w_qkv, swa_w_o, swa_sink, na_w_qkv, na_w_o, na_rel_bias, moe_router, moe_w_gate, moe_w_up, moe_w_down):
    return _forward(x, c, ctx, c_ctx, norm1_gain, norm2_gain, final_gain, w_ada, b_ada,
                    pool_w, pool_scale, conv_w_in, conv_w, conv_w_out,
                    swa_w_qkv, swa_w_o, swa_sink, na_w_qkv, na_w_o, na_rel_bias,
                    moe_router, moe_w_gate, moe_w_up, moe_w_down)
```

```python
import functools
from typing import NamedTuple

import numpy as np
import jax
import jax.numpy as jnp
from jax import lax
from jax.experimental import pallas as pl
from jax.experimental.pallas import tpu as pltpu

GRID_W = 64
HEAD_DIM = 128
KV_GROUP = 4
N_MOD = 6
EPS = 1e-6
NEG_INF = -1e30
LOG2E = 1.4426950408889634
POOL_WINDOWS = (2, 4, 8, 16)
CONV_W = 3
WINDOW = 128
Q_BLOCK = 128
ROPE_BASE = 10000.0
NA_KH_MAX = 8
NA_KW = 16
NA_ROWS_PER_BLOCK = 4
EC_CAPACITY_FACTOR = 2
LANE = 128
MOD_ROWS = 16
V7X_VMEM_LIMIT = 56 * 1024 * 1024

BF16 = jnp.bfloat16
F32 = jnp.float32


class Tiles(NamedTuple):
    norm_rows: int = 256
    mm_rows: int = 1024
    mm_cols: int = 1024
    res_cols: int = 512
    conv_cols: int = 512
    ada_cols: int = 512
    pool_cols: int = 512
    pool_chunk: int = 256
    gate_cols: int = 512
    swa_kv_heads: int = 4
    na_heads: int = 8
    moe_tokens: int = 256


def _params(vmem=V7X_VMEM_LIMIT):
    return pltpu.CompilerParams(vmem_limit_bytes=vmem)


def _dot(a, b):
    return jnp.dot(a, b, preferred_element_type=F32)


def _dot_nt(a, b):
    return lax.dot_general(a, b, (((1,), (1,)), ((), ())), preferred_element_type=F32)


def _ada_body(c_ref, w_ref, b_ref, o_ref):
    cvec = c_ref[...]
    a = (cvec * jax.nn.sigmoid(cvec)).astype(BF16)
    o_ref[...] = _dot(a, w_ref[...].astype(BF16)) + b_ref[...]


def _ada_call(cc, w_ada, b_ada, t):
    depth, d, nd = w_ada.shape
    bn = t.ada_cols
    return pl.pallas_call(
        _ada_body,
        grid=(depth, nd // bn),
        in_specs=[
            pl.BlockSpec((MOD_ROWS, d), lambda l, j: (0, 0)),
            pl.BlockSpec((None, d, bn), lambda l, j: (l, 0, j)),
            pl.BlockSpec((None, 1, bn), lambda l, j: (l, 0, j)),
        ],
        out_specs=pl.BlockSpec((None, MOD_ROWS, bn), lambda l, j: (l, 0, j)),
        out_shape=jax.ShapeDtypeStruct((depth, MOD_ROWS, nd), F32),
        compiler_params=_params(),
        name="adaln",
    )(cc, w_ada, b_ada.reshape(depth, 1, nd))


def _mod_spec(layer, k, cols, lat, bm, col_map=None, row_off=0):
    seq, bsz = lat
    sample = lambda i: jnp.minimum((row_off + i * bm) // seq, bsz)
    if col_map is None:
        return pl.BlockSpec((None, None, None, 1, cols), lambda i: (layer, sample(i), k, 0, 0))
    return pl.BlockSpec((None, None, None, 1, cols),
                        lambda j, i: (layer, sample(i), k, 0, col_map(j)))


def _norm_mod(x, g_ref, sh_ref, sc_ref):
    y = x * lax.rsqrt(jnp.mean(x * x, axis=-1, keepdims=True) + EPS)
    y = y * g_ref[...]
    return y * (1 + sc_ref[...]) + sh_ref[...]


def _norm_body(x_ref, g_ref, sh_ref, sc_ref, o_ref):
    o_ref[...] = _norm_mod(x_ref[...], g_ref, sh_ref, sc_ref).astype(o_ref.dtype)


def _router_body(x_ref, g_ref, sh_ref, sc_ref, wr_ref, aff_ref, *, n_experts):
    h2 = _norm_mod(x_ref[...], g_ref, sh_ref, sc_ref)
    logits = _dot(h2.astype(BF16), wr_ref[...])
    lane = lax.broadcasted_iota(jnp.int32, logits.shape, 1)
    logits = jnp.where(lane < n_experts, logits, NEG_INF)
    m = jnp.max(logits, axis=-1, keepdims=True)
    p = jnp.exp(logits - m)
    aff_ref[...] = p / jnp.sum(p, axis=-1, keepdims=True)


def _norm_call(x, gain, mods5, layer, k_shift, k_scale, lat, t, *, n_rows, out_dtype=None, w_router=None,
               n_experts=0, row_off=0):
    rows, d = x.shape
    bm = t.norm_rows
    in_specs = [pl.BlockSpec((bm, d), lambda i: (i, 0)),
                pl.BlockSpec((1, d), lambda i: (0, 0)),
                _mod_spec(layer, k_shift, d, lat, bm, row_off=row_off),
                _mod_spec(layer, k_scale, d, lat, bm, row_off=row_off)]
    args = [x, gain.reshape(1, d), mods5, mods5]
    if w_router is None:
        body, out_cols, name = _norm_body, d, "norm_mod"
    else:
        body, out_cols, out_dtype, name = functools.partial(_router_body, n_experts=n_experts), LANE, F32, "router"
        in_specs.append(pl.BlockSpec((d, LANE), lambda i: (0, 0)))
        args.append(w_router)
    return pl.pallas_call(
        body,
        grid=(n_rows // bm,),
        in_specs=in_specs,
        out_specs=pl.BlockSpec((bm, out_cols), lambda i: (i, 0)),
        out_shape=jax.ShapeDtypeStruct((n_rows, out_cols), out_dtype),
        compiler_params=_params(),
        name=name,
    )(*args)


def _mm_plain_body(x_ref, w_ref, o_ref):
    o_ref[...] = _dot(x_ref[...], w_ref[...]).astype(o_ref.dtype)


def _mm_plain(x, w, t, *, col_off=0, n_cols=None, out_dtype=BF16):
    rows, k = x.shape
    n_cols = w.shape[1] - col_off if n_cols is None else n_cols
    bm, bn = t.mm_rows, t.mm_cols
    off = col_off // bn
    return pl.pallas_call(
        _mm_plain_body,
        grid=(n_cols // bn, rows // bm),
        in_specs=[pl.BlockSpec((bm, k), lambda j, i: (i, 0)),
                  pl.BlockSpec((k, bn), lambda j, i: (0, off + j))],
        out_specs=pl.BlockSpec((bm, bn), lambda j, i: (i, j)),
        out_shape=jax.ShapeDtypeStruct((rows, n_cols), out_dtype),
        compiler_params=_params(),
        name="mm_plain",
    )(x, w)


def _rope_rotate(blk, lane):
    fwd = pltpu.roll(blk, LANE - HEAD_DIM // 4, 1)
    bwd = pltpu.roll(blk, HEAD_DIM // 4, 1)
    return jnp.where(lane % (HEAD_DIM // 2) < HEAD_DIM // 4, fwd, bwd)


def _mm_rope_body(x_ref, w_ref, cos_ref, sin_ref, o_ref, *, n_rope_blocks):
    acc = _dot(x_ref[...], w_ref[...])
    j = pl.program_id(0)

    @pl.when(j < n_rope_blocks)
    def _():
        cos, sin = cos_ref[...], sin_ref[...]
        lane = lax.broadcasted_iota(jnp.int32, cos.shape, 1)
        for hd in range(acc.shape[1] // HEAD_DIM):
            sl = slice(hd * HEAD_DIM, (hd + 1) * HEAD_DIM)
            blk = acc[:, sl]
            o_ref[:, sl] = (blk * cos + _rope_rotate(blk, lane) * sin).astype(o_ref.dtype)

    @pl.when(j >= n_rope_blocks)
    def _():
        o_ref[...] = acc.astype(o_ref.dtype)


def _mm_rope(x, w, cos, sin, n_rope_cols, t):
    rows, k = x.shape
    n = w.shape[1]
    bm, bn = t.mm_rows, t.mm_cols
    return pl.pallas_call(
        functools.partial(_mm_rope_body, n_rope_blocks=n_rope_cols // bn),
        grid=(n // bn, rows // bm),
        in_specs=[pl.BlockSpec((bm, k), lambda j, i: (i, 0)),
                  pl.BlockSpec((k, bn), lambda j, i: (0, j)),
                  pl.BlockSpec((bm, HEAD_DIM), lambda j, i: (i, 0)),
                  pl.BlockSpec((bm, HEAD_DIM), lambda j, i: (i, 0))],
        out_specs=pl.BlockSpec((bm, bn), lambda j, i: (i, j)),
        out_shape=jax.ShapeDtypeStruct((rows, n), BF16),
        compiler_params=_params(),
        name="mm_rope",
    )(x, w, cos, sin)


def _mm_conv_in_body(x_ref, wb_ref, wc_ref, wv_ref, b_ref, u_ref):
    x = x_ref[...]
    b_ref[...] = _dot(x, wb_ref[...]).astype(b_ref.dtype)
    u_ref[...] = (_dot(x, wc_ref[...]) * _dot(x, wv_ref[...])).astype(u_ref.dtype)


def _mm_conv_in(x, w_in, t):
    rows, k = x.shape
    d = w_in.shape[1] // 3
    bm, bn = t.mm_rows, t.conv_cols
    nb = d // bn
    wspec = lambda part: pl.BlockSpec((k, bn), lambda j, i: (0, part * nb + j))
    ospec = pl.BlockSpec((bm, bn), lambda j, i: (i, j))
    return pl.pallas_call(
        _mm_conv_in_body,
        grid=(nb, rows // bm),
        in_specs=[pl.BlockSpec((bm, k), lambda j, i: (i, 0)), wspec(0), wspec(1), wspec(2)],
        out_specs=[ospec, ospec],
        out_shape=[jax.ShapeDtypeStruct((rows, d), BF16)] * 2,
        compiler_params=_params(),
        name="mm_conv_in",
    )(x, w_in, w_in, w_in)


def _mm_res_body(*refs, n_x, n_res, first_blocks, col_scale):
    x_refs = refs[:n_x]
    w_ref = refs[n_x]
    res_refs = refs[n_x + 1:n_x + 1 + n_res]
    mod_ref = refs[n_x + 1 + n_res]
    o_ref = refs[-1]

    def emit(part):
        acc = _dot(x_refs[part][...], w_ref[...])
        if col_scale:
            acc = acc * refs[n_x + 2 + n_res][...]
        o_ref[...] = res_refs[min(part, n_res - 1)][...] + mod_ref[...] * acc

    if n_x == 1:
        emit(0)
    else:
        i = pl.program_id(1)
        pl.when(i < first_blocks)(lambda: emit(0))
        pl.when(i >= first_blocks)(lambda: emit(1))


def _mm_res(xs, w, res, mods5, layer, k_gate, lat, t, *, col_scale=None, grouped=False):
    k = xs[0].shape[1]
    bm = t.mm_rows
    part_blocks = [x.shape[0] // bm for x in xs]
    first = part_blocks[0]
    row_maps = [lambda i: jnp.minimum(i, first - 1), lambda i: jnp.maximum(i - first, 0)]
    in_place = not isinstance(res, (list, tuple))
    res_parts = [res] if in_place else list(res)
    d = res_parts[0].shape[1]
    if grouped:
        kg = w.shape[1]
        bn = kg
        x_specs = [pl.BlockSpec((bm, kg), lambda j, i, rm=rm: (rm(i), j)) for rm in row_maps[:len(xs)]]
        w_spec = pl.BlockSpec((None, kg, kg), lambda j, i: (j, 0, 0))
    else:
        bn = t.res_cols
        x_specs = [pl.BlockSpec((bm, k), lambda j, i, rm=rm: (rm(i), 0)) for rm in row_maps[:len(xs)]]
        w_spec = pl.BlockSpec((k, bn), lambda j, i: (0, j))
    if in_place:
        res_specs = [pl.BlockSpec((bm, bn), lambda j, i: (i, j))]
    else:
        assert [r.shape[0] // bm for r in res_parts] == part_blocks
        res_specs = [pl.BlockSpec((bm, bn), lambda j, i, rm=rm: (rm(i), j)) for rm in row_maps[:len(res_parts)]]
    in_specs = x_specs + [w_spec] + res_specs + [_mod_spec(layer, k_gate, bn, lat, bm, col_map=lambda j: j)]
    args = [*xs, w, *res_parts, mods5]
    if col_scale is not None:
        in_specs.append(pl.BlockSpec((1, bn), lambda j, i: (0, j)))
        args.append(col_scale.reshape(1, d))
    n_rows = sum(part_blocks) * bm
    return pl.pallas_call(
        functools.partial(_mm_res_body, n_x=len(xs), n_res=len(res_parts), first_blocks=first,
                          col_scale=col_scale is not None),
        grid=(d // bn, sum(part_blocks)),
        in_specs=in_specs,
        out_specs=pl.BlockSpec((bm, bn), lambda j, i: (i, j)),
        out_shape=jax.ShapeDtypeStruct(res.shape if in_place else (n_rows, d), F32),
        input_output_aliases={len(xs) + 1: 0} if in_place else {},
        compiler_params=_params(),
        name="mm_res",
    )(*args)


_POOL_HALO = 16


def _pool_body(h_ref, *rest, seq, chunk, groups_per_col):
    o_ref, pad_ref = rest[-2:]
    cols = h_ref.shape[1]
    zeros = jnp.zeros((_POOL_HALO, cols), F32)
    pad_ref[pl.ds(0, _POOL_HALO), :] = zeros
    pad_ref[pl.ds(_POOL_HALO + seq, _POOL_HALO), :] = zeros
    pad_ref[pl.ds(_POOL_HALO, seq), :] = h_ref[...]
    win_rows = chunk + 2 * _POOL_HALO
    group = pl.program_id(1) // groups_per_col

    for g, w in enumerate(POOL_WINDOWS):
        @pl.when(group == g)
        def _(w=w):
            def body(c, carry):
                start = pl.multiple_of(c * chunk, chunk)
                win = pad_ref[pl.ds(start, win_rows), :]
                s, width = win, 1
                while width < w:
                    s = s + pltpu.roll(s, width, 0)
                    width *= 2
                lead = w // 2 - 1
                if lead:
                    s = pltpu.roll(s, win_rows - lead, 0)
                tok = start + lax.broadcasted_iota(jnp.int32, (chunk, 1), 0)
                cnt = (jnp.minimum(tok + w // 2, seq) - jnp.maximum(tok - w // 2, 0)).astype(F32)
                centre = slice(_POOL_HALO, _POOL_HALO + chunk)
                o_ref[pl.ds(start, chunk), :] = (s[centre] / cnt - win[centre]).astype(o_ref.dtype)
                return carry
            lax.fori_loop(0, seq // chunk, body, 0)


def _pool_call(h, seq, n_seq, row_off, t):
    rows, d = h.shape
    bc = t.pool_cols
    chunk = min(t.pool_chunk, seq)
    group_cols = d // len(POOL_WINDOWS)
    off = row_off // seq
    return pl.pallas_call(
        functools.partial(_pool_body, seq=seq, chunk=chunk, groups_per_col=group_cols // bc),
        grid=(n_seq, d // bc),
        in_specs=[pl.BlockSpec((seq, bc), lambda b, j: (off + b, j))],
        out_specs=pl.BlockSpec((seq, bc), lambda b, j: (b, j)),
        out_shape=jax.ShapeDtypeStruct((n_seq * seq, d), BF16),
        scratch_shapes=[pltpu.VMEM((seq + 2 * _POOL_HALO, bc), F32)],
        compiler_params=_params(),
        name="pool",
    )(h)


def _conv_gate_body(b_ref, u_ref, w_ref, *rest, seq):
    o_ref = rest[-1]
    u = u_ref[...].astype(F32)
    row = lax.broadcasted_iota(jnp.int32, (seq, 1), 0)
    prev = jnp.where(row == 0, 0.0, pltpu.roll(u, 1, 0))
    nxt = jnp.where(row == seq - 1, 0.0, pltpu.roll(u, seq - 1, 0))
    conv = prev * w_ref[0:1, :] + u * w_ref[1:2, :] + nxt * w_ref[2:3, :]
    o_ref[...] = (b_ref[...].astype(F32) * conv).astype(o_ref.dtype)


def _conv_gate_call(bg, u, conv_w, seq, n_seq, row_off, t):
    rows, d = bg.shape
    bc = t.gate_cols
    off = row_off // seq
    spec = lambda: pl.BlockSpec((seq, bc), lambda b, j: (off + b, j))
    return pl.pallas_call(
        functools.partial(_conv_gate_body, seq=seq),
        grid=(n_seq, d // bc),
        in_specs=[spec(), spec(), pl.BlockSpec((CONV_W, bc), lambda b, j: (0, j))],
        out_specs=pl.BlockSpec((seq, bc), lambda b, j: (b, j)),
        out_shape=jax.ShapeDtypeStruct((n_seq * seq, d), BF16),
        compiler_params=_params(),
        name="conv_gate",
    )(bg, u, conv_w)


def _lane_fold(pieces, op):
    cols = [p[:, c:c + LANE] for p in pieces for c in range(0, p.shape[1], LANE)]
    acc = cols[0]
    for col in cols[1:]:
        acc = op(acc, col)
    return acc


def _attn_body(*refs, n_band, group, heads, scale, has_bias, bias_per_head, has_sink):
    q_ref = refs[0]
    pos = 1
    k_band = refs[pos:pos + n_band]
    v_band = refs[pos + n_band:pos + 2 * n_band]
    pos += 2 * n_band
    kx_ref, vx_ref = refs[pos], refs[pos + 1]
    pos += 2
    bias_ref = sink_ref = None
    if has_bias:
        bias_ref = refs[pos]
        pos += 1
    if has_sink:
        sink_ref = refs[pos]
        pos += 1
    o_ref = refs[-1]
    kv_block = pl.program_id(2) if has_sink else 0

    q_heads = [(hh, hh * group + g) for hh in range(heads) for g in range(group)]
    head_cols = lambda h: slice(h * HEAD_DIM, (h + 1) * HEAD_DIM)

    all_scores = []
    for hh, qh in q_heads:
        q = q_ref[:, head_cols(qh)]
        scores = [_dot_nt(q, r[:, head_cols(hh)]) * scale for r in (*k_band, kx_ref)]
        if has_bias:
            tk = k_band[0].shape[0]
            for bi in range(n_band):
                bsl = slice(bi * tk, (bi + 1) * tk)
                bias = bias_ref[hh, :, bsl] if bias_per_head else bias_ref[0, :, bsl]
                scores[bi] = scores[bi] + bias
        all_scores.append(scores)

    all_probs = []
    for (hh, qh), scores in zip(q_heads, all_scores):
        m = _lane_fold(scores, jnp.maximum).max(axis=-1, keepdims=True)
        if has_sink:
            sink = sink_ref[kv_block * heads * group + qh]
            m = jnp.maximum(m, sink)
        probs = [jnp.exp2(s - m) for s in scores]
        denom = _lane_fold(probs, jnp.add).sum(axis=-1, keepdims=True)
        if has_sink:
            denom = denom + jnp.exp2(sink - m)
        all_probs.append(([p.astype(BF16) for p in probs], denom))

    for (hh, qh), (probs, denom) in zip(q_heads, all_probs):
        vals = [r[:, head_cols(hh)] for r in (*v_band, vx_ref)]
        out = _dot(probs[0], vals[0])
        for p, v in zip(probs[1:], vals[1:]):
            out = out + _dot(p, v)
        o_ref[:, head_cols(qh)] = (out / denom).astype(o_ref.dtype)


def _attn_call(qkv, kvx, *, n_seq, seq, tq, n_q_heads, group, heads, q_col, k_col, v_col,
               kx_col, vx_col, x_row_off, x_len, q_row_off=0, band=True, bias=None,
               bias_per_head=False, sink=None, head_major=False):
    nb = seq // tq
    n_kv_heads = n_q_heads // group
    n_hblk = n_kv_heads // heads
    qw, kw = heads * group * HEAD_DIM, heads * HEAD_DIM
    n_band = 3 if band else 0
    if head_major:
        grid = (nb, n_hblk, n_seq)
        ids = lambda j, h, b: (b, j, h)
    else:
        grid = (n_seq, nb, n_hblk)
        ids = lambda b, j, h: (b, j, h)

    def wrap(f):
        return lambda *g: f(*ids(*g))

    q_blk0, x_blk0 = q_row_off // tq, x_row_off // x_len
    q_spec = pl.BlockSpec((tq, qw), wrap(lambda b, j, h: (q_blk0 + b * nb + j, q_col // qw + h)))
    in_specs, args = [q_spec], [qkv]

    def band_spec(col, shift):
        return pl.BlockSpec((tq, kw), wrap(
            lambda b, j, h: (q_blk0 + b * nb + jnp.clip(j + shift, 0, nb - 1), col // kw + h)))

    if band:
        for col in (k_col, v_col):
            for shift in (-1, 0, 1):
                in_specs.append(band_spec(col, shift))
                args.append(qkv)
    for col in (kx_col, vx_col):
        in_specs.append(pl.BlockSpec((x_len, kw), wrap(lambda b, j, h, col=col: (x_blk0 + b, col // kw + h))))
        args.append(kvx)
    if bias is not None:
        variant = lambda j: jnp.where(j == 0, 0, jnp.where(j == nb - 1, 2, 1))
        hb = heads if bias_per_head else 1
        in_specs.append(pl.BlockSpec((hb, None, tq, 3 * tq), wrap(
            lambda b, j, h: (h if bias_per_head else 0, variant(j), 0, 0))))
        args.append(bias * LOG2E)
    if sink is not None:
        in_specs.append(pl.BlockSpec(memory_space=pltpu.SMEM))
        args.append(sink * LOG2E)
    if sink is not None and head_major:
        raise ValueError("sink lookup assumes the head block is grid axis 2")
    return pl.pallas_call(
        functools.partial(_attn_body, n_band=n_band, group=group, heads=heads, scale=HEAD_DIM ** -0.5 * LOG2E,
                          has_bias=bias is not None, bias_per_head=bias_per_head, has_sink=sink is not None),
        grid=grid,
        in_specs=in_specs,
        out_specs=pl.BlockSpec((tq, qw), wrap(lambda b, j, h: (b * nb + j, h))),
        out_shape=jax.ShapeDtypeStruct((n_seq * seq, n_q_heads * HEAD_DIM), BF16),
        compiler_params=_params(),
        name="attn",
    )(*args)


def _swa_mask_table(seq):
    nb = seq // Q_BLOCK
    tables = []
    for j in (0, 1, nb - 1):
        qpos = j * Q_BLOCK + np.arange(Q_BLOCK)
        kpos = (j - 1) * Q_BLOCK + np.arange(3 * Q_BLOCK)
        valid = (np.abs(qpos[:, None] - kpos[None, :]) <= WINDOW) & ((kpos >= 0) & (kpos < seq))[None, :]
        tables.append(np.where(valid, 0.0, NEG_INF).astype(np.float32))
    return jnp.asarray(np.stack(tables)[None])


def _na_bias_table(rel_bias, seq):
    rows = seq // GRID_W
    kh = min(NA_KH_MAX, rows)
    rb_rows = NA_ROWS_PER_BLOCK
    nrb = rows // rb_rows
    rl, c = np.divmod(np.arange(rb_rows * GRID_W), GRID_W)
    kidx = np.arange(3 * rb_rows * GRID_W)
    kb, krem = np.divmod(kidx, rb_rows * GRID_W)
    krl, kc = np.divmod(krem, GRID_W)
    valid_all = []
    for rb in (0, 1, nrb - 1):
        r = rb_rows * rb + rl
        kr = rb_rows * (rb - 1 + kb) + krl
        r0 = np.clip(r - kh // 2, 0, rows - kh)
        c0 = np.clip(c - NA_KW // 2, 0, GRID_W - NA_KW)
        valid_all.append((kr[None, :] >= r0[:, None]) & (kr[None, :] < r0[:, None] + kh)
                         & (kc[None, :] >= c0[:, None]) & (kc[None, :] < c0[:, None] + NA_KW)
                         & (kr[None, :] >= 0) & (kr[None, :] < rows))
    valid = np.stack(valid_all)
    n_h, n_dr, n_dc = rel_bias.shape
    edge = GRID_W - NA_KW
    padded = jnp.pad(rel_bias.astype(F32), ((0, 0), (0, 0), (edge, edge)), mode="edge")
    toep = jnp.stack([padded[:, :, GRID_W - 1 - ci:2 * GRID_W - 1 - ci] for ci in range(GRID_W)], axis=2)
    dr0 = NA_KH_MAX - 1 - rb_rows
    assert dr0 - (rb_rows - 1) >= 0 and dr0 + 3 * rb_rows - 1 < n_dr
    blocks = jnp.stack([toep[:, dr0 - q:dr0 - q + 3 * rb_rows] for q in range(rb_rows)], axis=1)
    tbl = blocks.transpose(0, 1, 3, 2, 4).reshape(n_h, rb_rows * GRID_W, 3 * rb_rows * GRID_W)
    return jnp.where(jnp.asarray(valid)[None], tbl[:, None], NEG_INF)


def _rope_tables(seq, n_lat_rows, n_rows):
    half = HEAD_DIM // 4
    inv_freq = ROPE_BASE ** (-jnp.arange(half, dtype=F32) / half)
    t = jnp.arange(n_lat_rows) % seq
    ang_r = (t // GRID_W).astype(F32)[:, None] * inv_freq[None, :]
    ang_c = (t % GRID_W).astype(F32)[:, None] * inv_freq[None, :]
    cos = jnp.concatenate([jnp.cos(ang_r)] * 2 + [jnp.cos(ang_c)] * 2, axis=-1)
    sin = jnp.concatenate([-jnp.sin(ang_r), jnp.sin(ang_r), -jnp.sin(ang_c), jnp.sin(ang_c)], axis=-1)
    pad = n_rows - n_lat_rows
    cos = jnp.concatenate([cos, jnp.ones((pad, HEAD_DIM), F32)])
    sin = jnp.concatenate([sin, jnp.zeros((pad, HEAD_DIM), F32)])
    return cos, sin


_DMA_UNROLL = 8
_GATHER_CHUNKS = 8


def _pack_bf16_pair(y):
    half = y.shape[1] // 2
    bits = lambda v: lax.bitcast_convert_type(v.astype(BF16).astype(F32), jnp.uint32)
    return (bits(y[:, :half]) & jnp.uint32(0xFFFF0000)) | (bits(y[:, half:]) >> 16)


def _unpack_bf16_pair(p):
    as_bf16 = lambda v: lax.bitcast_convert_type(v, F32).astype(BF16)
    return as_bf16(p & jnp.uint32(0xFFFF0000)), as_bf16(p << 16)


def _expert_body(idx_ref, idx_next_ref, pos_ref, x_hbm, gate_ref, gain_ref, sh_ref, sc_ref,
                 wg_ref, wu_ref, wd_ref, ys_hbm, xs, ybuf, gsem, ssem, *, rows, n_steps):
    step = pl.program_id(0) * pl.num_programs(1) + pl.program_id(1)
    slot = step % 2

    def start_gather(iref, dst_slot):
        def body(c, carry):
            pltpu.make_async_copy(x_hbm.at[pl.ds(iref[0, c], 1), :], xs.at[dst_slot, pl.ds(c, 1), :],
                                  gsem.at[dst_slot]).start()
            return carry
        lax.fori_loop(0, rows, body, 0, unroll=_DMA_UNROLL)

    def wait_gather(s):
        pltpu.make_async_copy(x_hbm.at[pl.ds(0, rows), :], xs.at[s], gsem.at[s]).wait()

    def wait_scatter(s):
        pltpu.make_async_copy(ybuf.at[s], ys_hbm.at[pl.ds(0, rows), :], ssem.at[s]).wait()

    @pl.when(step == 0)
    def _():
        start_gather(idx_ref, 0)

    wait_gather(slot)
    h2 = _norm_mod(xs[slot], gain_ref, sh_ref, sc_ref).astype(BF16)
    d = h2.shape[1]
    kw, per = d // _GATHER_CHUNKS, rows // _GATHER_CHUNKS
    hg = hu = None
    for kc in range(_GATHER_CHUNKS):
        ksl = slice(kc * kw, (kc + 1) * kw)
        pg, pu = _dot(h2[:, ksl], wg_ref[ksl, :]), _dot(h2[:, ksl], wu_ref[ksl, :])
        hg, hu = (pg, pu) if hg is None else (hg + pg, hu + pu)
        for c in range(kc * per, (kc + 1) * per):
            pltpu.make_async_copy(x_hbm.at[pl.ds(idx_next_ref[0, c], 1), :],
                                  xs.at[1 - slot, pl.ds(c, 1), :], gsem.at[1 - slot]).start()
    hid = (hg * jax.nn.sigmoid(hg)) * hu
    out = _dot(hid.astype(BF16), wd_ref[...]) * gate_ref[...]

    @pl.when(step >= 2)
    def _():
        wait_scatter(slot)

    ybuf[slot] = _pack_bf16_pair(out)

    def scatter_body(c, carry):
        pltpu.make_async_copy(ybuf.at[slot, pl.ds(c, 1), :], ys_hbm.at[pl.ds(pos_ref[0, c], 1), :],
                              ssem.at[slot]).start()
        return carry
    lax.fori_loop(0, rows, scatter_body, 0, unroll=_DMA_UNROLL)

    @pl.when(step == n_steps - 1)
    def _():
        wait_gather(1 - slot)
        wait_scatter(slot)
        if n_steps > 1:
            wait_scatter(1 - slot)


def _expert_call(xa, gain, mods5, layer, idx_rows, pos_rows, gate, wg, wu, wd, n_groups, n_out_rows, *,
                 n_lat_groups):
    _, n_exp, d, ff = wg.shape
    rows = idx_rows.shape[-1]
    n_steps = n_exp * n_groups
    step_spec = lambda nxt: pl.BlockSpec(
        (None, 1, rows), lambda e, g: (jnp.minimum(e * n_groups + g + nxt, n_steps - 1), 0, 0),
        memory_space=pltpu.SMEM)
    mod_spec = lambda k: pl.BlockSpec((None, None, None, 1, d),
                                      lambda e, g: (layer, jnp.minimum(g, n_lat_groups), k, 0, 0))
    w_spec = lambda shape: pl.BlockSpec((None, None) + shape, lambda e, g: (layer, e, 0, 0))
    return pl.pallas_call(
        functools.partial(_expert_body, rows=rows, n_steps=n_steps),
        grid=(n_exp, n_groups),
        in_specs=[step_spec(0), step_spec(1), step_spec(0),
                  pl.BlockSpec(memory_space=pl.ANY),
                  pl.BlockSpec((None, rows, 1), lambda e, g: (e * n_groups + g, 0, 0)),
                  pl.BlockSpec((1, d), lambda e, g: (0, 0)),
                  mod_spec(3), mod_spec(4),
                  w_spec((d, ff)), w_spec((d, ff)), w_spec((ff, d))],
        out_specs=pl.BlockSpec(memory_space=pl.ANY),
        out_shape=jax.ShapeDtypeStruct((n_out_rows, d // 2), jnp.uint32),
        scratch_shapes=[pltpu.VMEM((2, rows, d), F32), pltpu.VMEM((2, rows, d // 2), jnp.uint32),
                        pltpu.SemaphoreType.DMA((2,)), pltpu.SemaphoreType.DMA((2,))],
        compiler_params=_params(),
        name="moe_expert",
    )(idx_rows, idx_rows, pos_rows, xa, gate, gain.reshape(1, d), mods5, mods5, wg, wu, wd)


_ITEM_FIRST, _ITEM_LAST, _ITEM_VALID = 1, 2, 4


def _combine_body(blk_ref, chunk_ref, flag_ref, tb_ref, modrow_ref, ys_ref, tok_ref, res_ref, mod_ref, *rest,
                  tokens, post):
    acc_ref = rest[-1]
    if post is None:
        o_ref = rest[-2]
    else:
        gain_ref, o_ref, h_ref = rest[0], rest[-3], rest[-2]
        if post == "norm_mod":
            sh_ref, sc_ref = rest[1], rest[2]
    b, t = pl.program_id(0), pl.program_id(1)
    flag = flag_ref[b, t]

    @pl.when((flag & _ITEM_FIRST) != 0)
    def _():
        acc_ref[...] = jnp.zeros_like(acc_ref)

    @pl.when((flag & _ITEM_VALID) != 0)
    def _():
        pairs = tok_ref.shape[-1]
        tok = tb_ref[b, t] * tokens + lax.broadcasted_iota(jnp.int32, (tokens, pairs), 0)
        sel = (tok_ref[...] == tok).astype(BF16)
        y_left, y_right = _unpack_bf16_pair(ys_ref[...])
        half = y_left.shape[1]
        acc_ref[:, :half] += _dot(sel, y_left)
        acc_ref[:, half:] += _dot(sel, y_right)

    @pl.when((flag & _ITEM_LAST) != 0)
    def _():
        xn = res_ref[...] + mod_ref[...] * acc_ref[...]
        o_ref[...] = xn
        if post is not None:
            y = xn * lax.rsqrt(jnp.mean(xn * xn, axis=-1, keepdims=True) + EPS)
            y = y * gain_ref[...]
            if post == "norm_mod":
                y = y * (1 + sc_ref[...]) + sh_ref[...]
            h_ref[...] = y.astype(h_ref.dtype)


class _Post(NamedTuple):
    gain: jax.Array
    layer: int | None
    dtype: object
    n_rows: int


def _combine_call(ys, tok_sorted, items, res, mods5, layer, k_gate, t, post):
    d = res.shape[1]
    tokens = t.moe_tokens
    n_sets, n_items = items[0].shape
    n_prefetch = len(items)
    mod_spec = lambda lyr, k: pl.BlockSpec((None, None, None, 1, d),
                                           lambda s, i, blk, ch, fl, tb, mr: (lyr, mr[s], k, 0, 0))
    row_spec = lambda: pl.BlockSpec((tokens, d), lambda s, i, blk, ch, fl, tb, mr: (blk[s, i], 0))
    in_specs = [
        pl.BlockSpec((tokens, d // 2), lambda s, i, blk, ch, fl, tb, mr: (ch[s, i], 0)),
        pl.BlockSpec((None, 1, tokens), lambda s, i, blk, ch, fl, tb, mr: (ch[s, i], 0, 0)),
        row_spec(), mod_spec(layer, k_gate),
        pl.BlockSpec((1, d), lambda s, i, blk, ch, fl, tb, mr: (0, 0))]
    args = [*items, ys, tok_sorted, res, mods5, post.gain.reshape(1, d)]
    mode = "norm" if post.layer is None else "norm_mod"
    if mode == "norm_mod":
        in_specs += [mod_spec(post.layer, 0), mod_spec(post.layer, 1)]
        args += [mods5, mods5]
    return pl.pallas_call(
        functools.partial(_combine_body, tokens=tokens, post=mode),
        grid_spec=pltpu.PrefetchScalarGridSpec(
            num_scalar_prefetch=n_prefetch, grid=(n_sets, n_items), in_specs=in_specs,
            out_specs=[row_spec(), row_spec()],
            scratch_shapes=[pltpu.VMEM((tokens, d), F32)]),
        out_shape=[jax.ShapeDtypeStruct(res.shape, F32), jax.ShapeDtypeStruct((post.n_rows, d), post.dtype)],
        input_output_aliases={n_prefetch + 2: 0},
        compiler_params=_params(),
        name="moe_combine",
    )(*args)


def _route(aff, n_sets, set_len, n_exp, row_off, pair_off, tokens):
    cap = EC_CAPACITY_FACTOR * set_len // n_exp
    n_pairs = n_exp * cap
    aff = aff[row_off:row_off + n_sets * set_len, :n_exp].reshape(n_sets, set_len, n_exp)
    gate, idx = lax.top_k(aff.transpose(0, 2, 1), cap)
    flat = idx.reshape(n_sets, n_pairs)
    order = jnp.argsort(flat, axis=1)
    tok_sorted = jnp.take_along_axis(flat, order, axis=1)
    pos = jnp.argsort(order, axis=1)
    set_id = jnp.arange(n_sets, dtype=jnp.int32)[:, None]
    idx_rows = (row_off + set_id * set_len + flat).reshape(n_sets, n_exp, cap).astype(jnp.int32)
    pos_rows = (pair_off + set_id * n_pairs + pos).reshape(n_sets, n_exp, cap).astype(jnp.int32)

    n_blk, n_chunks = set_len // tokens, n_pairs // tokens
    bounds = jnp.arange(n_blk + 1, dtype=jnp.int32) * tokens
    start = jnp.sum(tok_sorted[:, :, None] < bounds[None, None, :], axis=1).astype(jnp.int32)
    lo, hi = start[:, :-1], start[:, 1:]
    first_chunk = jnp.minimum(lo // tokens, n_chunks - 1)
    last_chunk = jnp.where(hi > lo, (hi - 1) // tokens, first_chunk)
    count = last_chunk - first_chunk + 1
    end = jnp.cumsum(count, axis=1)
    begin = end - count
    n_items = n_chunks + n_blk - 1
    item = jnp.arange(n_items, dtype=jnp.int32)
    tb = jnp.minimum(jnp.sum(end[:, None, :] <= item[None, :, None], axis=2), n_blk - 1).astype(jnp.int32)
    pick = lambda a: jnp.take_along_axis(a, tb, axis=1)
    valid = item[None, :] < end[:, -1:]
    chunk = jnp.minimum(pick(first_chunk) + item[None, :] - pick(begin), pick(last_chunk))
    is_first = valid & (item[None, :] == pick(begin))
    is_last = valid & (item[None, :] == pick(end) - 1)
    flag = (is_first * _ITEM_FIRST + is_last * _ITEM_LAST + valid * _ITEM_VALID).astype(jnp.int32)
    set_col = set_id.astype(jnp.int32)
    blk = row_off // tokens + set_col * n_blk + tb
    chunk = pair_off // tokens + set_col * n_chunks + chunk.astype(jnp.int32)
    items = (blk, chunk, flag, tb)
    return (idx_rows, pos_rows, gate), tok_sorted.astype(jnp.int32).reshape(n_sets * n_chunks, 1, tokens), items


def _expert_groups(per_stream, rows, n_real_pairs):
    cols = ([], [], [])
    n_spare = 0
    for idx, pos, gate in per_stream:
        n_sets, n_exp, cap = idx.shape
        per_exp = n_sets * cap
        pad = -per_exp % rows
        fills = (None, None, None)
        if pad:
            spare = n_real_pairs + n_spare + (jnp.arange(n_exp, dtype=jnp.int32)[:, None] * pad
                                              + jnp.arange(pad, dtype=jnp.int32)[None, :])
            n_spare += n_exp * pad
            fills = (jnp.broadcast_to(idx[0, 0, 0], (n_exp, pad)), spare, jnp.zeros((n_exp, pad), gate.dtype))
        for col, arr, fill in zip(cols, (idx, pos, gate), fills):
            flat = arr.transpose(1, 0, 2).reshape(n_exp, per_exp)
            if pad:
                flat = jnp.concatenate([flat, fill], axis=1)
            col.append(flat.reshape(n_exp, -1, rows))
    idx, pos, gate = (jnp.concatenate(col, axis=1) for col in cols)
    n_groups = idx.shape[1]
    return (idx.reshape(-1, 1, rows), pos.reshape(-1, 1, rows), gate.reshape(-1, rows, 1), n_groups,
            n_real_pairs + n_spare)


def _moe(xa, aff, streams, gain2, moe_w, mods5, layer, t, post):
    wg, wu, wd = moe_w
    n_exp = wg.shape[1]
    n_lat_sets = streams[0][0]
    routed, pair_off = [], 0
    for n_sets, set_len, row_off in streams:
        routed.append(_route(aff, n_sets, set_len, n_exp, row_off, pair_off, t.moe_tokens))
        pair_off += n_sets * n_exp * (EC_CAPACITY_FACTOR * set_len // n_exp)
    rows = EC_CAPACITY_FACTOR * streams[0][1] // n_exp
    idx_rows, pos_rows, gate, n_groups, n_out = _expert_groups([r[0] for r in routed], rows, pair_off)
    ys = _expert_call(xa, gain2, mods5, layer, idx_rows, pos_rows, gate, wg, wu, wd, n_groups, n_out,
                      n_lat_groups=n_lat_sets)
    n_items = max(r[2][0].shape[1] for r in routed)

    def pad_items(arr, is_flag):
        extra = n_items - arr.shape[1]
        if extra == 0:
            return arr
        tail = jnp.zeros((arr.shape[0], extra), arr.dtype) if is_flag else jnp.repeat(arr[:, -1:], extra, axis=1)
        return jnp.concatenate([arr, tail], axis=1)

    items = [jnp.concatenate([pad_items(r[2][k], k == 2) for r in routed], axis=0) for k in range(4)]
    mod_row = jnp.concatenate([jnp.arange(n_sets, dtype=jnp.int32) if si == 0
                               else jnp.full((n_sets,), n_lat_sets, jnp.int32)
                               for si, (n_sets, _, _) in enumerate(streams)])
    tok_sorted = jnp.concatenate([r[1] for r in routed], axis=0)
    return _combine_call(ys, tok_sorted, (*items, mod_row), xa, mods5, layer, 5, t, post)


def _forward(x, c, ctx, c_ctx, norm1_gain, norm2_gain, final_gain, w_ada, b_ada,
             pool_w, pool_scale, conv_w_in, conv_w, conv_w_out,
             swa_w_qkv, swa_w_o, swa_sink, na_w_qkv, na_w_o, na_rel_bias,
             moe_router, moe_w_gate, moe_w_up, moe_w_down, t=Tiles()):
    bsz, seq, d = x.shape
    n_ctx = ctx.shape[1]
    depth = w_ada.shape[0]
    n_exp = moe_router.shape[-1]
    n_lat, n_all = bsz * seq, bsz * (seq + n_ctx)
    n_heads = d // HEAD_DIM
    n_kv = n_heads // KV_GROUP
    kv_dim = n_kv * HEAD_DIM
    assert bsz + 1 <= MOD_ROWS and seq % t.mm_rows == 0 and n_all % t.mm_rows == 0

    cc = jnp.concatenate([c, c_ctx[None], jnp.zeros((MOD_ROWS - bsz - 1, d), F32)], axis=0)
    mods5 = _ada_call(cc, w_ada, b_ada, t).reshape(depth, MOD_ROWS, N_MOD, 1, d)
    w_router = jnp.pad(moe_router, ((0, 0), (0, 0), (0, LANE - n_exp))).astype(BF16)
    moe_w = (moe_w_gate.astype(BF16), moe_w_up.astype(BF16), moe_w_down.astype(BF16))
    lat_stream, ctx_stream = (bsz, seq, 0), (bsz, n_ctx, n_lat)
    lat = (seq, bsz)
    h_dtype = lambda layer: F32 if layer % 4 == 0 else BF16

    parts0 = [x.reshape(n_lat, d)] + ([ctx.reshape(bsz * n_ctx, d)] if depth > 1 else [])
    xa = h = None
    for i in range(depth):
        m, j = i % 4, i // 4
        need_ctx = i < depth - 1
        rows_out = n_all if need_ctx else n_lat
        streams = [lat_stream, ctx_stream] if need_ctx else [lat_stream]

        if i == 0:
            h_parts = [_norm_call(p, norm1_gain[0], mods5, 0, 0, 1, lat, t, n_rows=p.shape[0],
                                  out_dtype=h_dtype(0), row_off=off)
                       for p, (_, _, off) in zip(parts0, streams)]
            y_in = [_pool_call(hp, s_len, n_s, 0, t) for hp, (n_s, s_len, _) in zip(h_parts, streams)]
            xa = _mm_res(y_in, pool_w[j].astype(BF16), parts0, mods5, i, 2, lat, t,
                         col_scale=pool_scale[j], grouped=True)
        elif m == 0:
            y_in = [_pool_call(h, s_len, n_s, off, t) for n_s, s_len, off in streams]
            xa = _mm_res(y_in, pool_w[j].astype(BF16), xa, mods5, i, 2, lat, t,
                         col_scale=pool_scale[j], grouped=True)
        elif m == 1:
            bg, u = _mm_conv_in(h, conv_w_in[j].astype(BF16), t)
            y_in = [_conv_gate_call(bg, u, conv_w[j], s_len, n_s, off, t) for n_s, s_len, off in streams]
            xa = _mm_res(y_in, conv_w_out[j].astype(BF16), xa, mods5, i, 2, lat, t)
        else:
            if m == 2:
                cos, sin = _rope_tables(seq, n_lat, n_all)
                qkv = _mm_rope(h, swa_w_qkv[j].astype(BF16), cos, sin, d + kv_dim, t)
                common = dict(n_seq=bsz, n_q_heads=n_heads, group=KV_GROUP, heads=t.swa_kv_heads,
                              q_col=0, k_col=d, v_col=d + kv_dim, kx_col=d, vx_col=d + kv_dim,
                              x_row_off=n_lat, x_len=n_ctx, sink=swa_sink[j])
                lat_kw = dict(tq=Q_BLOCK, bias=_swa_mask_table(seq))
                w_o = swa_w_o[j]
            else:
                qkv = _mm_plain(h, na_w_qkv[j].astype(BF16), t)
                common = dict(n_seq=bsz, n_q_heads=n_heads, group=1, heads=t.na_heads,
                              q_col=0, k_col=d, v_col=2 * d, kx_col=d, vx_col=2 * d,
                              x_row_off=n_lat, x_len=n_ctx)
                lat_kw = dict(tq=NA_ROWS_PER_BLOCK * GRID_W, bias=_na_bias_table(na_rel_bias[j], seq),
                              bias_per_head=True, head_major=True)
                w_o = na_w_o[j]
            y_in = [_attn_call(qkv, qkv, seq=seq, **lat_kw, **common)]
            if need_ctx:
                y_in.append(_attn_call(qkv, qkv, seq=n_ctx, tq=n_ctx, q_row_off=n_lat, band=False, **common))
            xa = _mm_res(y_in, w_o.astype(BF16), xa, mods5, i, 2, lat, t)

        aff = _norm_call(xa, norm2_gain[i], mods5, i, 3, 4, lat, t, n_rows=rows_out,
                         w_router=w_router[i], n_experts=n_exp)
        post = (_Post(norm1_gain[i + 1], i + 1, h_dtype(i + 1), n_all) if need_ctx
                else _Post(final_gain, None, F32, n_lat))
        xa, h = _moe(xa, aff, streams, norm2_gain[i], moe_w, mods5, i, t, post)

    return h.reshape(bsz, seq, d)


def kernel(x, c, ctx, c_ctx, norm1_gain, norm2_gain, final_gain, w_ada, b_ada, pool_w, pool_scale, conv_w_in, conv_w, conv_w_out, swa_w_qkv, swa_w_o, swa_sink, na_w_qkv, na_w_o, na_rel_bias, moe_router, moe_w_gate, moe_w_up, moe_w_down):
    return _forward(x, c, ctx, c_ctx, norm1_gain, norm2_gain, final_gain, w_ada, b_ada,
                    pool_w, pool_scale, conv_w_in, conv_w, conv_w_out,
                    swa_w_qkv, swa_w_o, swa_sink, na_w_qkv, na_w_o, na_rel_bias,
                    moe_router, moe_w_gate, moe_w_up, moe_w_down)
```

```python
import functools
from typing import NamedTuple

import numpy as np
import jax
import jax.numpy as jnp
from jax import lax
from jax.experimental import pallas as pl
from jax.experimental.pallas import tpu as pltpu

GRID_W = 64
HEAD_DIM = 128
KV_GROUP = 4
N_MOD = 6
EPS = 1e-6
NEG_INF = -1e30
LOG2E = 1.4426950408889634
POOL_WINDOWS = (2, 4, 8, 16)
CONV_W = 3
WINDOW = 128
Q_BLOCK = 128
ROPE_BASE = 10000.0
NA_KH_MAX = 8
NA_KW = 16
NA_ROWS_PER_BLOCK = 4
EC_CAPACITY_FACTOR = 2
LANE = 128
MOD_ROWS = 16
V7X_VMEM_LIMIT = 56 * 1024 * 1024

BF16 = jnp.bfloat16
F32 = jnp.float32


class Tiles(NamedTuple):
    norm_rows: int = 256
    mm_rows: int = 1024
    mm_cols: int = 1024
    res_cols: int = 512
    conv_cols: int = 512
    ada_cols: int = 512
    pool_cols: int = 512
    pool_chunk: int = 256
    gate_cols: int = 512
    swa_kv_heads: int = 4
    na_heads: int = 8
    moe_tokens: int = 256


def _params(vmem=V7X_VMEM_LIMIT):
    return pltpu.CompilerParams(vmem_limit_bytes=vmem)


def _dot(a, b):
    return jnp.dot(a, b, preferred_element_type=F32)


def _dot_nt(a, b):
    return lax.dot_general(a, b, (((1,), (1,)), ((), ())), preferred_element_type=F32)


def _ada_body(c_ref, w_ref, b_ref, o_ref):
    cvec = c_ref[...]
    a = (cvec * jax.nn.sigmoid(cvec)).astype(BF16)
    o_ref[...] = _dot(a, w_ref[...].astype(BF16)) + b_ref[...]


def _ada_call(cc, w_ada, b_ada, t):
    depth, d, nd = w_ada.shape
    bn = t.ada_cols
    return pl.pallas_call(
        _ada_body,
        grid=(depth, nd // bn),
        in_specs=[
            pl.BlockSpec((MOD_ROWS, d), lambda l, j: (0, 0)),
            pl.BlockSpec((None, d, bn), lambda l, j: (l, 0, j)),
            pl.BlockSpec((None, 1, bn), lambda l, j: (l, 0, j)),
        ],
        out_specs=pl.BlockSpec((None, MOD_ROWS, bn), lambda l, j: (l, 0, j)),
        out_shape=jax.ShapeDtypeStruct((depth, MOD_ROWS, nd), F32),
        compiler_params=_params(),
        name="adaln",
    )(cc, w_ada, b_ada.reshape(depth, 1, nd))


def _mod_spec(layer, k, cols, lat, bm, col_map=None, row_off=0):
    seq, bsz = lat
    sample = lambda i: jnp.minimum((row_off + i * bm) // seq, bsz)
    if col_map is None:
        return pl.BlockSpec((None, None, None, 1, cols), lambda i: (layer, sample(i), k, 0, 0))
    return pl.BlockSpec((None, None, None, 1, cols),
                        lambda j, i: (layer, sample(i), k, 0, col_map(j)))


def _norm_mod(x, g_ref, sh_ref, sc_ref):
    y = x * lax.rsqrt(jnp.mean(x * x, axis=-1, keepdims=True) + EPS)
    y = y * g_ref[...]
    return y * (1 + sc_ref[...]) + sh_ref[...]


def _norm_body(x_ref, g_ref, sh_ref, sc_ref, o_ref):
    o_ref[...] = _norm_mod(x_ref[...], g_ref, sh_ref, sc_ref).astype(o_ref.dtype)


def _router_body(x_ref, g_ref, sh_ref, sc_ref, wr_ref, aff_ref, *, n_experts):
    h2 = _norm_mod(x_ref[...], g_ref, sh_ref, sc_ref)
    logits = _dot(h2.astype(BF16), wr_ref[...])
    lane = lax.broadcasted_iota(jnp.int32, logits.shape, 1)
    logits = jnp.where(lane < n_experts, logits, NEG_INF)
    m = jnp.max(logits, axis=-1, keepdims=True)
    p = jnp.exp(logits - m)
    aff_ref[...] = p / jnp.sum(p, axis=-1, keepdims=True)


def _norm_call(x, gain, mods5, layer, k_shift, k_scale, lat, t, *, n_rows, out_dtype=None, w_router=None,
               n_experts=0, row_off=0):
    rows, d = x.shape
    bm = t.norm_rows
    in_specs = [pl.BlockSpec((bm, d), lambda i: (i, 0)),
                pl.BlockSpec((1, d), lambda i: (0, 0)),
                _mod_spec(layer, k_shift, d, lat, bm, row_off=row_off),
                _mod_spec(layer, k_scale, d, lat, bm, row_off=row_off)]
    args = [x, gain.reshape(1, d), mods5, mods5]
    if w_router is None:
        body, out_cols, name = _norm_body, d, "norm_mod"
    else:
        body, out_cols, out_dtype, name = functools.partial(_router_body, n_experts=n_experts), LANE, F32, "router"
        in_specs.append(pl.BlockSpec((d, LANE), lambda i: (0, 0)))
        args.append(w_router)
    return pl.pallas_call(
        body,
        grid=(n_rows // bm,),
        in_specs=in_specs,
        out_specs=pl.BlockSpec((bm, out_cols), lambda i: (i, 0)),
        out_shape=jax.ShapeDtypeStruct((n_rows, out_cols), out_dtype),
        compiler_params=_params(),
        name=name,
    )(*args)


def _mm_plain_body(x_ref, w_ref, o_ref):
    o_ref[...] = _dot(x_ref[...], w_ref[...]).astype(o_ref.dtype)


def _mm_plain(x, w, t, *, col_off=0, n_cols=None, out_dtype=BF16):
    rows, k = x.shape
    n_cols = w.shape[1] - col_off if n_cols is None else n_cols
    bm, bn = t.mm_rows, t.mm_cols
    off = col_off // bn
    return pl.pallas_call(
        _mm_plain_body,
        grid=(n_cols // bn, rows // bm),
        in_specs=[pl.BlockSpec((bm, k), lambda j, i: (i, 0)),
                  pl.BlockSpec((k, bn), lambda j, i: (0, off + j))],
        out_specs=pl.BlockSpec((bm, bn), lambda j, i: (i, j)),
        out_shape=jax.ShapeDtypeStruct((rows, n_cols), out_dtype),
        compiler_params=_params(),
        name="mm_plain",
    )(x, w)


def _rope_rotate(blk, lane):
    fwd = pltpu.roll(blk, LANE - HEAD_DIM // 4, 1)
    bwd = pltpu.roll(blk, HEAD_DIM // 4, 1)
    return jnp.where(lane % (HEAD_DIM // 2) < HEAD_DIM // 4, fwd, bwd)


def _mm_rope_body(x_ref, w_ref, cos_ref, sin_ref, o_ref, *, n_rope_blocks):
    acc = _dot(x_ref[...], w_ref[...])
    j = pl.program_id(0)

    @pl.when(j < n_rope_blocks)
    def _():
        cos, sin = cos_ref[...], sin_ref[...]
        lane = lax.broadcasted_iota(jnp.int32, cos.shape, 1)
        for hd in range(acc.shape[1] // HEAD_DIM):
            sl = slice(hd * HEAD_DIM, (hd + 1) * HEAD_DIM)
            blk = acc[:, sl]
            o_ref[:, sl] = (blk * cos + _rope_rotate(blk, lane) * sin).astype(o_ref.dtype)

    @pl.when(j >= n_rope_blocks)
    def _():
        o_ref[...] = acc.astype(o_ref.dtype)


def _mm_rope(x, w, cos, sin, n_rope_cols, t):
    rows, k = x.shape
    n = w.shape[1]
    bm, bn = t.mm_rows, t.mm_cols
    return pl.pallas_call(
        functools.partial(_mm_rope_body, n_rope_blocks=n_rope_cols // bn),
        grid=(n // bn, rows // bm),
        in_specs=[pl.BlockSpec((bm, k), lambda j, i: (i, 0)),
                  pl.BlockSpec((k, bn), lambda j, i: (0, j)),
                  pl.BlockSpec((bm, HEAD_DIM), lambda j, i: (i, 0)),
                  pl.BlockSpec((bm, HEAD_DIM), lambda j, i: (i, 0))],
        out_specs=pl.BlockSpec((bm, bn), lambda j, i: (i, j)),
        out_shape=jax.ShapeDtypeStruct((rows, n), BF16),
        compiler_params=_params(),
        name="mm_rope",
    )(x, w, cos, sin)


def _mm_conv_in_body(x_ref, wb_ref, wc_ref, wv_ref, b_ref, u_ref):
    x = x_ref[...]
    b_ref[...] = _dot(x, wb_ref[...]).astype(b_ref.dtype)
    u_ref[...] = (_dot(x, wc_ref[...]) * _dot(x, wv_ref[...])).astype(u_ref.dtype)


def _mm_conv_in(x, w_in, t):
    rows, k = x.shape
    d = w_in.shape[1] // 3
    bm, bn = t.mm_rows, t.conv_cols
    nb = d // bn
    wspec = lambda part: pl.BlockSpec((k, bn), lambda j, i: (0, part * nb + j))
    ospec = pl.BlockSpec((bm, bn), lambda j, i: (i, j))
    return pl.pallas_call(
        _mm_conv_in_body,
        grid=(nb, rows // bm),
        in_specs=[pl.BlockSpec((bm, k), lambda j, i: (i, 0)), wspec(0), wspec(1), wspec(2)],
        out_specs=[ospec, ospec],
        out_shape=[jax.ShapeDtypeStruct((rows, d), BF16)] * 2,
        compiler_params=_params(),
        name="mm_conv_in",
    )(x, w_in, w_in, w_in)


def _mm_res_body(*refs, n_x, n_res, first_blocks, col_scale):
    x_refs = refs[:n_x]
    w_ref = refs[n_x]
    res_refs = refs[n_x + 1:n_x + 1 + n_res]
    mod_ref = refs[n_x + 1 + n_res]
    o_ref = refs[-1]

    def emit(part):
        acc = _dot(x_refs[part][...], w_ref[...])
        if col_scale:
            acc = acc * refs[n_x + 2 + n_res][...]
        o_ref[...] = res_refs[min(part, n_res - 1)][...] + mod_ref[...] * acc

    if n_x == 1:
        emit(0)
    else:
        i = pl.program_id(1)
        pl.when(i < first_blocks)(lambda: emit(0))
        pl.when(i >= first_blocks)(lambda: emit(1))


def _mm_res(xs, w, res, mods5, layer, k_gate, lat, t, *, col_scale=None, grouped=False):
    k = xs[0].shape[1]
    bm = t.mm_rows
    part_blocks = [x.shape[0] // bm for x in xs]
    first = part_blocks[0]
    row_maps = [lambda i: jnp.minimum(i, first - 1), lambda i: jnp.maximum(i - first, 0)]
    in_place = not isinstance(res, (list, tuple))
    res_parts = [res] if in_place else list(res)
    d = res_parts[0].shape[1]
    if grouped:
        kg = w.shape[1]
        bn = kg
        x_specs = [pl.BlockSpec((bm, kg), lambda j, i, rm=rm: (rm(i), j)) for rm in row_maps[:len(xs)]]
        w_spec = pl.BlockSpec((None, kg, kg), lambda j, i: (j, 0, 0))
    else:
        bn = t.res_cols
        x_specs = [pl.BlockSpec((bm, k), lambda j, i, rm=rm: (rm(i), 0)) for rm in row_maps[:len(xs)]]
        w_spec = pl.BlockSpec((k, bn), lambda j, i: (0, j))
    if in_place:
        res_specs = [pl.BlockSpec((bm, bn), lambda j, i: (i, j))]
    else:
        assert [r.shape[0] // bm for r in res_parts] == part_blocks
        res_specs = [pl.BlockSpec((bm, bn), lambda j, i, rm=rm: (rm(i), j)) for rm in row_maps[:len(res_parts)]]
    in_specs = x_specs + [w_spec] + res_specs + [_mod_spec(layer, k_gate, bn, lat, bm, col_map=lambda j: j)]
    args = [*xs, w, *res_parts, mods5]
    if col_scale is not None:
        in_specs.append(pl.BlockSpec((1, bn), lambda j, i: (0, j)))
        args.append(col_scale.reshape(1, d))
    n_rows = sum(part_blocks) * bm
    return pl.pallas_call(
        functools.partial(_mm_res_body, n_x=len(xs), n_res=len(res_parts), first_blocks=first,
                          col_scale=col_scale is not None),
        grid=(d // bn, sum(part_blocks)),
        in_specs=in_specs,
        out_specs=pl.BlockSpec((bm, bn), lambda j, i: (i, j)),
        out_shape=jax.ShapeDtypeStruct(res.shape if in_place else (n_rows, d), F32),
        input_output_aliases={len(xs) + 1: 0} if in_place else {},
        compiler_params=_params(),
        name="mm_res",
    )(*args)


_POOL_HALO = 16


def _pool_body(h_ref, *rest, seq, chunk, groups_per_col):
    o_ref, pad_ref = rest[-2:]
    cols = h_ref.shape[1]
    zeros = jnp.zeros((_POOL_HALO, cols), F32)
    pad_ref[pl.ds(0, _POOL_HALO), :] = zeros
    pad_ref[pl.ds(_POOL_HALO + seq, _POOL_HALO), :] = zeros
    pad_ref[pl.ds(_POOL_HALO, seq), :] = h_ref[...]
    win_rows = chunk + 2 * _POOL_HALO
    group = pl.program_id(1) // groups_per_col

    for g, w in enumerate(POOL_WINDOWS):
        @pl.when(group == g)
        def _(w=w):
            def body(c, carry):
                start = pl.multiple_of(c * chunk, chunk)
                win = pad_ref[pl.ds(start, win_rows), :]
                s, width = win, 1
                while width < w:
                    s = s + pltpu.roll(s, width, 0)
                    width *= 2
                lead = w // 2 - 1
                if lead:
                    s = pltpu.roll(s, win_rows - lead, 0)
                tok = start + lax.broadcasted_iota(jnp.int32, (chunk, 1), 0)
                cnt = (jnp.minimum(tok + w // 2, seq) - jnp.maximum(tok - w // 2, 0)).astype(F32)
                centre = slice(_POOL_HALO, _POOL_HALO + chunk)
                o_ref[pl.ds(start, chunk), :] = (s[centre] / cnt - win[centre]).astype(o_ref.dtype)
                return carry
            lax.fori_loop(0, seq // chunk, body, 0)


def _pool_call(h, seq, n_seq, row_off, t):
    rows, d = h.shape
    bc = t.pool_cols
    chunk = min(t.pool_chunk, seq)
    group_cols = d // len(POOL_WINDOWS)
    off = row_off // seq
    return pl.pallas_call(
        functools.partial(_pool_body, seq=seq, chunk=chunk, groups_per_col=group_cols // bc),
        grid=(n_seq, d // bc),
        in_specs=[pl.BlockSpec((seq, bc), lambda b, j: (off + b, j))],
        out_specs=pl.BlockSpec((seq, bc), lambda b, j: (b, j)),
        out_shape=jax.ShapeDtypeStruct((n_seq * seq, d), BF16),
        scratch_shapes=[pltpu.VMEM((seq + 2 * _POOL_HALO, bc), F32)],
        compiler_params=_params(),
        name="pool",
    )(h)


def _conv_gate_body(b_ref, u_ref, w_ref, *rest, seq):
    o_ref = rest[-1]
    u = u_ref[...].astype(F32)
    row = lax.broadcasted_iota(jnp.int32, (seq, 1), 0)
    prev = jnp.where(row == 0, 0.0, pltpu.roll(u, 1, 0))
    nxt = jnp.where(row == seq - 1, 0.0, pltpu.roll(u, seq - 1, 0))
    conv = prev * w_ref[0:1, :] + u * w_ref[1:2, :] + nxt * w_ref[2:3, :]
    o_ref[...] = (b_ref[...].astype(F32) * conv).astype(o_ref.dtype)


def _conv_gate_call(bg, u, conv_w, seq, n_seq, row_off, t):
    rows, d = bg.shape
    bc = t.gate_cols
    off = row_off // seq
    spec = lambda: pl.BlockSpec((seq, bc), lambda b, j: (off + b, j))
    return pl.pallas_call(
        functools.partial(_conv_gate_body, seq=seq),
        grid=(n_seq, d // bc),
        in_specs=[spec(), spec(), pl.BlockSpec((CONV_W, bc), lambda b, j: (0, j))],
        out_specs=pl.BlockSpec((seq, bc), lambda b, j: (b, j)),
        out_shape=jax.ShapeDtypeStruct((n_seq * seq, d), BF16),
        compiler_params=_params(),
        name="conv_gate",
    )(bg, u, conv_w)


def _lane_fold(pieces, op):
    cols = [p[:, c:c + LANE] for p in pieces for c in range(0, p.shape[1], LANE)]
    acc = cols[0]
    for col in cols[1:]:
        acc = op(acc, col)
    return acc


def _attn_body(*refs, n_band, group, heads, scale, has_bias, bias_per_head, has_sink):
    q_ref = refs[0]
    pos = 1
    k_band = refs[pos:pos + n_band]
    v_band = refs[pos + n_band:pos + 2 * n_band]
    pos += 2 * n_band
    kx_ref, vx_ref = refs[pos], refs[pos + 1]
    pos += 2
    bias_ref = sink_ref = None
    if has_bias:
        bias_ref = refs[pos]
        pos += 1
    if has_sink:
        sink_ref = refs[pos]
        pos += 1
    o_ref = refs[-1]
    kv_block = pl.program_id(2) if has_sink else 0

    q_heads = [(hh, hh * group + g) for hh in range(heads) for g in range(group)]
    head_cols = lambda h: slice(h * HEAD_DIM, (h + 1) * HEAD_DIM)

    all_scores = []
    for hh, qh in q_heads:
        q = q_ref[:, head_cols(qh)]
        scores = [_dot_nt(q, r[:, head_cols(hh)]) * scale for r in (*k_band, kx_ref)]
        if has_bias:
            tk = k_band[0].shape[0]
            for bi in range(n_band):
                bsl = slice(bi * tk, (bi + 1) * tk)
                bias = bias_ref[hh, :, bsl] if bias_per_head else bias_ref[0, :, bsl]
                scores[bi] = scores[bi] + bias
        all_scores.append(scores)

    all_probs = []
    for (hh, qh), scores in zip(q_heads, all_scores):
        m = _lane_fold(scores, jnp.maximum).max(axis=-1, keepdims=True)
        if has_sink:
            sink = sink_ref[kv_block * heads * group + qh]
            m = jnp.maximum(m, sink)
        probs = [jnp.exp2(s - m) for s in scores]
        denom = _lane_fold(probs, jnp.add).sum(axis=-1, keepdims=True)
        if has_sink:
            denom = denom + jnp.exp2(sink - m)
        all_probs.append(([p.astype(BF16) for p in probs], denom))

    for (hh, qh), (probs, denom) in zip(q_heads, all_probs):
        vals = [r[:, head_cols(hh)] for r in (*v_band, vx_ref)]
        out = _dot(probs[0], vals[0])
        for p, v in zip(probs[1:], vals[1:]):
            out = out + _dot(p, v)
        o_ref[:, head_cols(qh)] = (out / denom).astype(o_ref.dtype)


def _attn_call(qkv, kvx, *, n_seq, seq, tq, n_q_heads, group, heads, q_col, k_col, v_col,
               kx_col, vx_col, x_row_off, x_len, q_row_off=0, band=True, bias=None,
               bias_per_head=False, sink=None, head_major=False):
    nb = seq // tq
    n_kv_heads = n_q_heads // group
    n_hblk = n_kv_heads // heads
    qw, kw = heads * group * HEAD_DIM, heads * HEAD_DIM
    n_band = 3 if band else 0
    if head_major:
        grid = (nb, n_hblk, n_seq)
        ids = lambda j, h, b: (b, j, h)
    else:
        grid = (n_seq, nb, n_hblk)
        ids = lambda b, j, h: (b, j, h)

    def wrap(f):
        return lambda *g: f(*ids(*g))

    q_blk0, x_blk0 = q_row_off // tq, x_row_off // x_len
    q_spec = pl.BlockSpec((tq, qw), wrap(lambda b, j, h: (q_blk0 + b * nb + j, q_col // qw + h)))
    in_specs, args = [q_spec], [qkv]

    def band_spec(col, shift):
        return pl.BlockSpec((tq, kw), wrap(
            lambda b, j, h: (q_blk0 + b * nb + jnp.clip(j + shift, 0, nb - 1), col // kw + h)))

    if band:
        for col in (k_col, v_col):
            for shift in (-1, 0, 1):
                in_specs.append(band_spec(col, shift))
                args.append(qkv)
    for col in (kx_col, vx_col):
        in_specs.append(pl.BlockSpec((x_len, kw), wrap(lambda b, j, h, col=col: (x_blk0 + b, col // kw + h))))
        args.append(kvx)
    if bias is not None:
        variant = lambda j: jnp.where(j == 0, 0, jnp.where(j == nb - 1, 2, 1))
        hb = heads if bias_per_head else 1
        in_specs.append(pl.BlockSpec((hb, None, tq, 3 * tq), wrap(
            lambda b, j, h: (h if bias_per_head else 0, variant(j), 0, 0))))
        args.append(bias * LOG2E)
    if sink is not None:
        in_specs.append(pl.BlockSpec(memory_space=pltpu.SMEM))
        args.append(sink * LOG2E)
    if sink is not None and head_major:
        raise ValueError("sink lookup assumes the head block is grid axis 2")
    return pl.pallas_call(
        functools.partial(_attn_body, n_band=n_band, group=group, heads=heads, scale=HEAD_DIM ** -0.5 * LOG2E,
                          has_bias=bias is not None, bias_per_head=bias_per_head, has_sink=sink is not None),
        grid=grid,
        in_specs=in_specs,
        out_specs=pl.BlockSpec((tq, qw), wrap(lambda b, j, h: (b * nb + j, h))),
        out_shape=jax.ShapeDtypeStruct((n_seq * seq, n_q_heads * HEAD_DIM), BF16),
        compiler_params=_params(),
        name="attn",
    )(*args)


def _swa_mask_table(seq):
    nb = seq // Q_BLOCK
    tables = []
    for j in (0, 1, nb - 1):
        qpos = j * Q_BLOCK + np.arange(Q_BLOCK)
        kpos = (j - 1) * Q_BLOCK + np.arange(3 * Q_BLOCK)
        valid = (np.abs(qpos[:, None] - kpos[None, :]) <= WINDOW) & ((kpos >= 0) & (kpos < seq))[None, :]
        tables.append(np.where(valid, 0.0, NEG_INF).astype(np.float32))
    return jnp.asarray(np.stack(tables)[None])


def _na_bias_table(rel_bias, seq):
    rows = seq // GRID_W
    kh = min(NA_KH_MAX, rows)
    rb_rows = NA_ROWS_PER_BLOCK
    nrb = rows // rb_rows
    rl, c = np.divmod(np.arange(rb_rows * GRID_W), GRID_W)
    kidx = np.arange(3 * rb_rows * GRID_W)
    kb, krem = np.divmod(kidx, rb_rows * GRID_W)
    krl, kc = np.divmod(krem, GRID_W)
    valid_all = []
    for rb in (0, 1, nrb - 1):
        r = rb_rows * rb + rl
        kr = rb_rows * (rb - 1 + kb) + krl
        r0 = np.clip(r - kh // 2, 0, rows - kh)
        c0 = np.clip(c - NA_KW // 2, 0, GRID_W - NA_KW)
        valid_all.append((kr[None, :] >= r0[:, None]) & (kr[None, :] < r0[:, None] + kh)
                         & (kc[None, :] >= c0[:, None]) & (kc[None, :] < c0[:, None] + NA_KW)
                         & (kr[None, :] >= 0) & (kr[None, :] < rows))
    valid = np.stack(valid_all)
    n_h, n_dr, n_dc = rel_bias.shape
    edge = GRID_W - NA_KW
    padded = jnp.pad(rel_bias.astype(F32), ((0, 0), (0, 0), (edge, edge)), mode="edge")
    toep = jnp.stack([padded[:, :, GRID_W - 1 - ci:2 * GRID_W - 1 - ci] for ci in range(GRID_W)], axis=2)
    dr0 = NA_KH_MAX - 1 - rb_rows
    assert dr0 - (rb_rows - 1) >= 0 and dr0 + 3 * rb_rows - 1 < n_dr
    blocks = jnp.stack([toep[:, dr0 - q:dr0 - q + 3 * rb_rows] for q in range(rb_rows)], axis=1)
    tbl = blocks.transpose(0, 1, 3, 2, 4).reshape(n_h, rb_rows * GRID_W, 3 * rb_rows * GRID_W)
    return jnp.where(jnp.asarray(valid)[None], tbl[:, None], NEG_INF)


def _rope_tables(seq, n_lat_rows, n_rows):
    half = HEAD_DIM // 4
    inv_freq = ROPE_BASE ** (-jnp.arange(half, dtype=F32) / half)
    t = jnp.arange(n_lat_rows) % seq
    ang_r = (t // GRID_W).astype(F32)[:, None] * inv_freq[None, :]
    ang_c = (t % GRID_W).astype(F32)[:, None] * inv_freq[None, :]
    cos = jnp.concatenate([jnp.cos(ang_r)] * 2 + [jnp.cos(ang_c)] * 2, axis=-1)
    sin = jnp.concatenate([-jnp.sin(ang_r), jnp.sin(ang_r), -jnp.sin(ang_c), jnp.sin(ang_c)], axis=-1)
    pad = n_rows - n_lat_rows
    cos = jnp.concatenate([cos, jnp.ones((pad, HEAD_DIM), F32)])
    sin = jnp.concatenate([sin, jnp.zeros((pad, HEAD_DIM), F32)])
    return cos, sin


_DMA_UNROLL = 8
_GATHER_CHUNKS = 8
_SCATTER_CHUNKS = 4


def _pack_bf16_pair(left, right):
    bits = lambda v: lax.bitcast_convert_type(v.astype(BF16).astype(F32), jnp.uint32)
    return (bits(left) & jnp.uint32(0xFFFF0000)) | (bits(right) >> 16)


def _unpack_bf16_pair(p):
    as_bf16 = lambda v: lax.bitcast_convert_type(v, F32).astype(BF16)
    return as_bf16(p & jnp.uint32(0xFFFF0000)), as_bf16(p << 16)


def _expert_body(idx_ref, idx_next_ref, pos_prev_ref, pos_ref, x_hbm, gate_ref, gain_ref, sh_ref, sc_ref,
                 wg_ref, wu_ref, wd_ref, ys_hbm, xs, ybuf, gsem, ssem, *, rows, n_steps):
    step = pl.program_id(0) * pl.num_programs(1) + pl.program_id(1)
    slot = step % 2

    def start_gather(iref, dst_slot):
        def body(c, carry):
            pltpu.make_async_copy(x_hbm.at[pl.ds(iref[0, c], 1), :], xs.at[dst_slot, pl.ds(c, 1), :],
                                  gsem.at[dst_slot]).start()
            return carry
        lax.fori_loop(0, rows, body, 0, unroll=_DMA_UNROLL)

    def wait_gather(s):
        pltpu.make_async_copy(x_hbm.at[pl.ds(0, rows), :], xs.at[s], gsem.at[s]).wait()

    def wait_scatter(s):
        pltpu.make_async_copy(ybuf.at[s], ys_hbm.at[pl.ds(0, rows), :], ssem.at[s]).wait()

    @pl.when(step == 0)
    def _():
        start_gather(idx_ref, 0)
        ybuf[1] = jnp.zeros(ybuf.shape[1:], ybuf.dtype)

    wait_gather(slot)
    h2 = _norm_mod(xs[slot], gain_ref, sh_ref, sc_ref).astype(BF16)
    d = h2.shape[1]
    kw, per = d // _GATHER_CHUNKS, rows // _GATHER_CHUNKS
    hg = hu = None
    for kc in range(_GATHER_CHUNKS):
        ksl = slice(kc * kw, (kc + 1) * kw)
        pg, pu = _dot(h2[:, ksl], wg_ref[ksl, :]), _dot(h2[:, ksl], wu_ref[ksl, :])
        hg, hu = (pg, pu) if hg is None else (hg + pg, hu + pu)
        for c in range(kc * per, (kc + 1) * per):
            pltpu.make_async_copy(x_hbm.at[pl.ds(idx_next_ref[0, c], 1), :],
                                  xs.at[1 - slot, pl.ds(c, 1), :], gsem.at[1 - slot]).start()
    hid = ((hg * jax.nn.sigmoid(hg)) * hu).astype(BF16)

    @pl.when(step >= 1)
    def _():
        wait_scatter(slot)

    half = d // 2
    cw, per = half // _SCATTER_CHUNKS, rows // _SCATTER_CHUNKS
    gate = gate_ref[...]
    for pc in range(_SCATTER_CHUNKS):
        lsl, rsl = slice(pc * cw, (pc + 1) * cw), slice(half + pc * cw, half + (pc + 1) * cw)
        ybuf[slot, :, lsl] = _pack_bf16_pair(_dot(hid, wd_ref[:, lsl]) * gate, _dot(hid, wd_ref[:, rsl]) * gate)
        for c in range(pc * per, (pc + 1) * per):
            pltpu.make_async_copy(ybuf.at[1 - slot, pl.ds(c, 1), :], ys_hbm.at[pl.ds(pos_prev_ref[0, c], 1), :],
                                  ssem.at[1 - slot]).start()

    @pl.when(step == n_steps - 1)
    def _():
        def scatter_body(c, carry):
            pltpu.make_async_copy(ybuf.at[slot, pl.ds(c, 1), :], ys_hbm.at[pl.ds(pos_ref[0, c], 1), :],
                                  ssem.at[slot]).start()
            return carry
        lax.fori_loop(0, rows, scatter_body, 0, unroll=_DMA_UNROLL)
        wait_gather(1 - slot)
        wait_scatter(1 - slot)
        wait_scatter(slot)


def _expert_call(xa, gain, mods5, layer, idx_rows, pos_rows, gate, wg, wu, wd, n_groups, n_out_rows, *,
                 n_lat_groups):
    _, n_exp, d, ff = wg.shape
    rows = idx_rows.shape[-1]
    n_steps = n_exp * n_groups
    spare = n_out_rows + jnp.arange(rows, dtype=jnp.int32).reshape(1, 1, rows)
    pos_rows = jnp.concatenate([spare, pos_rows], axis=0)
    n_out_rows += rows
    step_spec = lambda nxt, last: pl.BlockSpec(
        (None, 1, rows), lambda e, g: (jnp.minimum(e * n_groups + g + nxt, last), 0, 0),
        memory_space=pltpu.SMEM)
    mod_spec = lambda k: pl.BlockSpec((None, None, None, 1, d),
                                      lambda e, g: (layer, jnp.minimum(g, n_lat_groups), k, 0, 0))
    w_spec = lambda shape: pl.BlockSpec((None, None) + shape, lambda e, g: (layer, e, 0, 0))
    return pl.pallas_call(
        functools.partial(_expert_body, rows=rows, n_steps=n_steps),
        grid=(n_exp, n_groups),
        in_specs=[step_spec(0, n_steps - 1), step_spec(1, n_steps - 1),
                  step_spec(0, n_steps), step_spec(1, n_steps),
                  pl.BlockSpec(memory_space=pl.ANY),
                  pl.BlockSpec((None, rows, 1), lambda e, g: (e * n_groups + g, 0, 0)),
                  pl.BlockSpec((1, d), lambda e, g: (0, 0)),
                  mod_spec(3), mod_spec(4),
                  w_spec((d, ff)), w_spec((d, ff)), w_spec((ff, d))],
        out_specs=pl.BlockSpec(memory_space=pl.ANY),
        out_shape=jax.ShapeDtypeStruct((n_out_rows, d // 2), jnp.uint32),
        scratch_shapes=[pltpu.VMEM((2, rows, d), F32), pltpu.VMEM((2, rows, d // 2), jnp.uint32),
                        pltpu.SemaphoreType.DMA((2,)), pltpu.SemaphoreType.DMA((2,))],
        compiler_params=_params(),
        name="moe_expert",
    )(idx_rows, idx_rows, pos_rows, pos_rows, xa, gate, gain.reshape(1, d), mods5, mods5, wg, wu, wd)


_ITEM_FIRST, _ITEM_LAST, _ITEM_VALID = 1, 2, 4


def _combine_body(blk_ref, chunk_ref, flag_ref, tb_ref, modrow_ref, ys_ref, tok_ref, res_ref, mod_ref, *rest,
                  tokens, post):
    acc_ref = rest[-1]
    if post is None:
        o_ref = rest[-2]
    else:
        gain_ref, o_ref, h_ref = rest[0], rest[-3], rest[-2]
        if post == "norm_mod":
            sh_ref, sc_ref = rest[1], rest[2]
    t = pl.program_id(0)
    flag = flag_ref[t]

    @pl.when((flag & _ITEM_FIRST) != 0)
    def _():
        acc_ref[...] = jnp.zeros_like(acc_ref)

    @pl.when((flag & _ITEM_VALID) != 0)
    def _():
        pairs = tok_ref.shape[-1]
        tok = tb_ref[t] * tokens + lax.broadcasted_iota(jnp.int32, (tokens, pairs), 0)
        sel = (tok_ref[...] == tok).astype(BF16)
        y_left, y_right = _unpack_bf16_pair(ys_ref[...])
        half = y_left.shape[1]
        acc_ref[:, :half] += _dot(sel, y_left)
        acc_ref[:, half:] += _dot(sel, y_right)

    @pl.when((flag & _ITEM_LAST) != 0)
    def _():
        xn = res_ref[...] + mod_ref[...] * acc_ref[...]
        o_ref[...] = xn
        if post is not None:
            y = xn * lax.rsqrt(jnp.mean(xn * xn, axis=-1, keepdims=True) + EPS)
            y = y * gain_ref[...]
            if post == "norm_mod":
                y = y * (1 + sc_ref[...]) + sh_ref[...]
            h_ref[...] = y.astype(h_ref.dtype)


class _Post(NamedTuple):
    gain: jax.Array
    layer: int | None
    dtype: object
    n_rows: int


def _combine_call(ys, tok_sorted, items, res, mods5, layer, k_gate, t, post):
    d = res.shape[1]
    tokens = t.moe_tokens
    n_items = items[0].shape[0]
    n_prefetch = len(items)
    mod_spec = lambda lyr, k: pl.BlockSpec((None, None, None, 1, d),
                                           lambda i, blk, ch, fl, tb, mr: (lyr, mr[i], k, 0, 0))
    row_spec = lambda: pl.BlockSpec((tokens, d), lambda i, blk, ch, fl, tb, mr: (blk[i], 0))
    in_specs = [
        pl.BlockSpec((tokens, d // 2), lambda i, blk, ch, fl, tb, mr: (ch[i], 0)),
        pl.BlockSpec((None, 1, tokens), lambda i, blk, ch, fl, tb, mr: (ch[i], 0, 0)),
        row_spec(), mod_spec(layer, k_gate),
        pl.BlockSpec((1, d), lambda i, blk, ch, fl, tb, mr: (0, 0))]
    args = [*items, ys, tok_sorted, res, mods5, post.gain.reshape(1, d)]
    mode = "norm" if post.layer is None else "norm_mod"
    if mode == "norm_mod":
        in_specs += [mod_spec(post.layer, 0), mod_spec(post.layer, 1)]
        args += [mods5, mods5]
    return pl.pallas_call(
        functools.partial(_combine_body, tokens=tokens, post=mode),
        grid_spec=pltpu.PrefetchScalarGridSpec(
            num_scalar_prefetch=n_prefetch, grid=(n_items,), in_specs=in_specs,
            out_specs=[row_spec(), row_spec()],
            scratch_shapes=[pltpu.VMEM((tokens, d), F32)]),
        out_shape=[jax.ShapeDtypeStruct(res.shape, F32), jax.ShapeDtypeStruct((post.n_rows, d), post.dtype)],
        input_output_aliases={n_prefetch + 2: 0},
        compiler_params=_params(),
        name="moe_combine",
    )(*args)


def _route(aff, n_sets, set_len, n_exp, row_off, pair_off, tokens):
    cap = EC_CAPACITY_FACTOR * set_len // n_exp
    n_pairs = n_exp * cap
    aff = aff[row_off:row_off + n_sets * set_len, :n_exp].reshape(n_sets, set_len, n_exp)
    gate, idx = lax.top_k(aff.transpose(0, 2, 1), cap)
    flat = idx.reshape(n_sets, n_pairs)
    order = jnp.argsort(flat, axis=1)
    tok_sorted = jnp.take_along_axis(flat, order, axis=1)
    pos = jnp.argsort(order, axis=1)
    set_id = jnp.arange(n_sets, dtype=jnp.int32)[:, None]
    idx_rows = (row_off + set_id * set_len + flat).reshape(n_sets, n_exp, cap).astype(jnp.int32)
    pos_rows = (pair_off + set_id * n_pairs + pos).reshape(n_sets, n_exp, cap).astype(jnp.int32)

    n_blk, n_chunks = set_len // tokens, n_pairs // tokens
    bounds = jnp.arange(n_blk + 1, dtype=jnp.int32) * tokens
    start = jnp.sum(tok_sorted[:, :, None] < bounds[None, None, :], axis=1).astype(jnp.int32)
    lo, hi = start[:, :-1], start[:, 1:]
    first_chunk = jnp.minimum(lo // tokens, n_chunks - 1)
    last_chunk = jnp.where(hi > lo, (hi - 1) // tokens, first_chunk)
    count = last_chunk - first_chunk + 1
    end = jnp.cumsum(count, axis=1)
    begin = end - count
    n_items = n_chunks + n_blk - 1
    item = jnp.arange(n_items, dtype=jnp.int32)
    tb = jnp.minimum(jnp.sum(end[:, None, :] <= item[None, :, None], axis=2), n_blk - 1).astype(jnp.int32)
    pick = lambda a: jnp.take_along_axis(a, tb, axis=1)
    valid = item[None, :] < end[:, -1:]
    chunk = jnp.minimum(pick(first_chunk) + item[None, :] - pick(begin), pick(last_chunk))
    is_first = valid & (item[None, :] == pick(begin))
    is_last = valid & (item[None, :] == pick(end) - 1)
    flag = (is_first * _ITEM_FIRST + is_last * _ITEM_LAST + valid * _ITEM_VALID).astype(jnp.int32)
    set_col = set_id.astype(jnp.int32)
    blk = row_off // tokens + set_col * n_blk + tb
    chunk = pair_off // tokens + set_col * n_chunks + chunk.astype(jnp.int32)
    items = (blk, chunk, flag, tb)
    return (idx_rows, pos_rows, gate), tok_sorted.astype(jnp.int32).reshape(n_sets * n_chunks, 1, tokens), items


def _expert_groups(per_stream, rows, n_real_pairs):
    cols = ([], [], [])
    n_spare = 0
    for idx, pos, gate in per_stream:
        n_sets, n_exp, cap = idx.shape
        per_exp = n_sets * cap
        pad = -per_exp % rows
        fills = (None, None, None)
        if pad:
            spare = n_real_pairs + n_spare + (jnp.arange(n_exp, dtype=jnp.int32)[:, None] * pad
                                              + jnp.arange(pad, dtype=jnp.int32)[None, :])
            n_spare += n_exp * pad
            fills = (jnp.broadcast_to(idx[0, 0, 0], (n_exp, pad)), spare, jnp.zeros((n_exp, pad), gate.dtype))
        for col, arr, fill in zip(cols, (idx, pos, gate), fills):
            flat = arr.transpose(1, 0, 2).reshape(n_exp, per_exp)
            if pad:
                flat = jnp.concatenate([flat, fill], axis=1)
            col.append(flat.reshape(n_exp, -1, rows))
    idx, pos, gate = (jnp.concatenate(col, axis=1) for col in cols)
    n_groups = idx.shape[1]
    return (idx.reshape(-1, 1, rows), pos.reshape(-1, 1, rows), gate.reshape(-1, rows, 1), n_groups,
            n_real_pairs + n_spare)


def _moe(xa, aff, streams, gain2, moe_w, mods5, layer, t, post):
    wg, wu, wd = moe_w
    n_exp = wg.shape[1]
    n_lat_sets = streams[0][0]
    routed, pair_off = [], 0
    for n_sets, set_len, row_off in streams:
        routed.append(_route(aff, n_sets, set_len, n_exp, row_off, pair_off, t.moe_tokens))
        pair_off += n_sets * n_exp * (EC_CAPACITY_FACTOR * set_len // n_exp)
    rows = EC_CAPACITY_FACTOR * streams[0][1] // n_exp
    idx_rows, pos_rows, gate, n_groups, n_out = _expert_groups([r[0] for r in routed], rows, pair_off)
    ys = _expert_call(xa, gain2, mods5, layer, idx_rows, pos_rows, gate, wg, wu, wd, n_groups, n_out,
                      n_lat_groups=n_lat_sets)
    items = [jnp.concatenate([r[2][k].reshape(-1) for r in routed]) for k in range(4)]
    mod_row = jnp.concatenate([
        jnp.repeat(jnp.arange(n_sets, dtype=jnp.int32) if si == 0 else jnp.full((n_sets,), n_lat_sets, jnp.int32),
                   r[2][0].shape[1])
        for si, ((n_sets, _, _), r) in enumerate(zip(streams, routed))])
    tok_sorted = jnp.concatenate([r[1] for r in routed], axis=0)
    return _combine_call(ys, tok_sorted, (*items, mod_row), xa, mods5, layer, 5, t, post)


def _forward(x, c, ctx, c_ctx, norm1_gain, norm2_gain, final_gain, w_ada, b_ada,
             pool_w, pool_scale, conv_w_in, conv_w, conv_w_out,
             swa_w_qkv, swa_w_o, swa_sink, na_w_qkv, na_w_o, na_rel_bias,
             moe_router, moe_w_gate, moe_w_up, moe_w_down, t=Tiles()):
    bsz, seq, d = x.shape
    n_ctx = ctx.shape[1]
    depth = w_ada.shape[0]
    n_exp = moe_router.shape[-1]
    n_lat, n_all = bsz * seq, bsz * (seq + n_ctx)
    n_heads = d // HEAD_DIM
    n_kv = n_heads // KV_GROUP
    kv_dim = n_kv * HEAD_DIM
    assert bsz + 1 <= MOD_ROWS and seq % t.mm_rows == 0 and n_all % t.mm_rows == 0

    cc = jnp.concatenate([c, c_ctx[None], jnp.zeros((MOD_ROWS - bsz - 1, d), F32)], axis=0)
    mods5 = _ada_call(cc, w_ada, b_ada, t).reshape(depth, MOD_ROWS, N_MOD, 1, d)
    w_router = jnp.pad(moe_router, ((0, 0), (0, 0), (0, LANE - n_exp))).astype(BF16)
    moe_w = (moe_w_gate.astype(BF16), moe_w_up.astype(BF16), moe_w_down.astype(BF16))
    lat_stream, ctx_stream = (bsz, seq, 0), (bsz, n_ctx, n_lat)
    lat = (seq, bsz)
    h_dtype = lambda layer: F32 if layer % 4 == 0 else BF16

    parts0 = [x.reshape(n_lat, d)] + ([ctx.reshape(bsz * n_ctx, d)] if depth > 1 else [])
    xa = h = None
    for i in range(depth):
        m, j = i % 4, i // 4
        need_ctx = i < depth - 1
        rows_out = n_all if need_ctx else n_lat
        streams = [lat_stream, ctx_stream] if need_ctx else [lat_stream]

        if i == 0:
            h_parts = [_norm_call(p, norm1_gain[0], mods5, 0, 0, 1, lat, t, n_rows=p.shape[0],
                                  out_dtype=h_dtype(0), row_off=off)
                       for p, (_, _, off) in zip(parts0, streams)]
            y_in = [_pool_call(hp, s_len, n_s, 0, t) for hp, (n_s, s_len, _) in zip(h_parts, streams)]
            xa = _mm_res(y_in, pool_w[j].astype(BF16), parts0, mods5, i, 2, lat, t,
                         col_scale=pool_scale[j], grouped=True)
        elif m == 0:
            y_in = [_pool_call(h, s_len, n_s, off, t) for n_s, s_len, off in streams]
            xa = _mm_res(y_in, pool_w[j].astype(BF16), xa, mods5, i, 2, lat, t,
                         col_scale=pool_scale[j], grouped=True)
        elif m == 1:
            bg, u = _mm_conv_in(h, conv_w_in[j].astype(BF16), t)
            y_in = [_conv_gate_call(bg, u, conv_w[j], s_len, n_s, off, t) for n_s, s_len, off in streams]
            xa = _mm_res(y_in, conv_w_out[j].astype(BF16), xa, mods5, i, 2, lat, t)
        else:
            if m == 2:
                cos, sin = _rope_tables(seq, n_lat, n_all)
                qkv = _mm_rope(h, swa_w_qkv[j].astype(BF16), cos, sin, d + kv_dim, t)
                common = dict(n_seq=bsz, n_q_heads=n_heads, group=KV_GROUP, heads=t.swa_kv_heads,
                              q_col=0, k_col=d, v_col=d + kv_dim, kx_col=d, vx_col=d + kv_dim,
                              x_row_off=n_lat, x_len=n_ctx, sink=swa_sink[j])
                lat_kw = dict(tq=Q_BLOCK, bias=_swa_mask_table(seq))
                w_o = swa_w_o[j]
            else:
                qkv = _mm_plain(h, na_w_qkv[j].astype(BF16), t)
                common = dict(n_seq=bsz, n_q_heads=n_heads, group=1, heads=t.na_heads,
                              q_col=0, k_col=d, v_col=2 * d, kx_col=d, vx_col=2 * d,
                              x_row_off=n_lat, x_len=n_ctx)
                lat_kw = dict(tq=NA_ROWS_PER_BLOCK * GRID_W, bias=_na_bias_table(na_rel_bias[j], seq),
                              bias_per_head=True, head_major=True)
                w_o = na_w_o[j]
            y_in = [_attn_call(qkv, qkv, seq=seq, **lat_kw, **common)]
            if need_ctx:
                y_in.append(_attn_call(qkv, qkv, seq=n_ctx, tq=n_ctx, q_row_off=n_lat, band=False, **common))
            xa = _mm_res(y_in, w_o.astype(BF16), xa, mods5, i, 2, lat, t)

        aff = _norm_call(xa, norm2_gain[i], mods5, i, 3, 4, lat, t, n_rows=rows_out,
                         w_router=w_router[i], n_experts=n_exp)
        post = (_Post(norm1_gain[i + 1], i + 1, h_dtype(i + 1), n_all) if need_ctx
                else _Post(final_gain, None, F32, n_lat))
        xa, h = _moe(xa, aff, streams, norm2_gain[i], moe_w, mods5, i, t, post)

    return h.reshape(bsz, seq, d)


def kernel(x, c, ctx, c_ctx, norm1_gain, norm2_gain, final_gain, w_ada, b_ada, pool_w, pool_scale, conv_w_in, conv_w, conv_w_out, swa_w_qkv, swa_w_o, swa_sink, na_w_qkv, na_w_o, na_rel_bias, moe_router, moe_w_gate, moe_w_up, moe_w_down):
    return _forward(x, c, ctx, c_ctx, norm1_gain, norm2_gain, final_gain, w_ada, b_ada,
                    pool_w, pool_scale, conv_w_in, conv_w, conv_w_out,
                    swa_w_qkv, swa_w_o, swa_sink, na_w_qkv, na_w_o, na_rel_bias,
                    moe_router, moe_w_gate, moe_w_up, moe_w_down)
```

```python
import functools
from typing import NamedTuple

import numpy as np
import jax
import jax.numpy as jnp
from jax import lax
from jax.experimental import pallas as pl
from jax.experimental.pallas import tpu as pltpu

GRID_W = 64
HEAD_DIM = 128
KV_GROUP = 4
N_MOD = 6
EPS = 1e-6
NEG_INF = -1e30
LOG2E = 1.4426950408889634
POOL_WINDOWS = (2, 4, 8, 16)
CONV_W = 3
WINDOW = 128
Q_BLOCK = 128
ROPE_BASE = 10000.0
NA_KH_MAX = 8
NA_KW = 16
NA_ROWS_PER_BLOCK = 4
EC_CAPACITY_FACTOR = 2
LANE = 128
MOD_ROWS = 16
V7X_VMEM_LIMIT = 56 * 1024 * 1024

BF16 = jnp.bfloat16
F32 = jnp.float32


class Tiles(NamedTuple):
    norm_rows: int = 256
    mm_rows: int = 1024
    mm_cols: int = 1024
    res_cols: int = 512
    conv_cols: int = 512
    ada_cols: int = 512
    pool_cols: int = 512
    pool_chunk: int = 256
    gate_cols: int = 512
    swa_kv_heads: int = 4
    na_heads: int = 8
    moe_tokens: int = 256


def _params(vmem=V7X_VMEM_LIMIT):
    return pltpu.CompilerParams(vmem_limit_bytes=vmem)


def _dot(a, b):
    return jnp.dot(a, b, preferred_element_type=F32)


def _dot_nt(a, b):
    return lax.dot_general(a, b, (((1,), (1,)), ((), ())), preferred_element_type=F32)


def _ada_body(c_ref, w_ref, b_ref, o_ref):
    cvec = c_ref[...]
    a = (cvec * jax.nn.sigmoid(cvec)).astype(BF16)
    o_ref[...] = _dot(a, w_ref[...].astype(BF16)) + b_ref[...]


def _ada_call(cc, w_ada, b_ada, t):
    depth, d, nd = w_ada.shape
    bn = t.ada_cols
    return pl.pallas_call(
        _ada_body,
        grid=(depth, nd // bn),
        in_specs=[
            pl.BlockSpec((MOD_ROWS, d), lambda l, j: (0, 0)),
            pl.BlockSpec((None, d, bn), lambda l, j: (l, 0, j)),
            pl.BlockSpec((None, 1, bn), lambda l, j: (l, 0, j)),
        ],
        out_specs=pl.BlockSpec((None, MOD_ROWS, bn), lambda l, j: (l, 0, j)),
        out_shape=jax.ShapeDtypeStruct((depth, MOD_ROWS, nd), F32),
        compiler_params=_params(),
        name="adaln",
    )(cc, w_ada, b_ada.reshape(depth, 1, nd))


def _mod_spec(layer, k, cols, lat, bm, col_map=None, row_off=0):
    seq, bsz = lat
    sample = lambda i: jnp.minimum((row_off + i * bm) // seq, bsz)
    if col_map is None:
        return pl.BlockSpec((None, None, None, 1, cols), lambda i: (layer, sample(i), k, 0, 0))
    return pl.BlockSpec((None, None, None, 1, cols),
                        lambda j, i: (layer, sample(i), k, 0, col_map(j)))


def _norm_mod(x, g_ref, sh_ref, sc_ref):
    y = x * lax.rsqrt(jnp.mean(x * x, axis=-1, keepdims=True) + EPS)
    y = y * g_ref[...]
    return y * (1 + sc_ref[...]) + sh_ref[...]


def _norm_body(x_ref, g_ref, sh_ref, sc_ref, o_ref):
    o_ref[...] = _norm_mod(x_ref[...], g_ref, sh_ref, sc_ref).astype(o_ref.dtype)


def _router_body(x_ref, g_ref, sh_ref, sc_ref, wr_ref, aff_ref, *, n_experts):
    h2 = _norm_mod(x_ref[...], g_ref, sh_ref, sc_ref)
    logits = _dot(h2.astype(BF16), wr_ref[...])
    lane = lax.broadcasted_iota(jnp.int32, logits.shape, 1)
    logits = jnp.where(lane < n_experts, logits, NEG_INF)
    m = jnp.max(logits, axis=-1, keepdims=True)
    p = jnp.exp(logits - m)
    aff_ref[...] = p / jnp.sum(p, axis=-1, keepdims=True)


def _norm_call(x, gain, mods5, layer, k_shift, k_scale, lat, t, *, n_rows, out_dtype=None, w_router=None,
               n_experts=0, row_off=0):
    rows, d = x.shape
    bm = t.norm_rows
    in_specs = [pl.BlockSpec((bm, d), lambda i: (i, 0)),
                pl.BlockSpec((1, d), lambda i: (0, 0)),
                _mod_spec(layer, k_shift, d, lat, bm, row_off=row_off),
                _mod_spec(layer, k_scale, d, lat, bm, row_off=row_off)]
    args = [x, gain.reshape(1, d), mods5, mods5]
    if w_router is None:
        body, out_cols, name = _norm_body, d, "norm_mod"
    else:
        body, out_cols, out_dtype, name = functools.partial(_router_body, n_experts=n_experts), LANE, F32, "router"
        in_specs.append(pl.BlockSpec((d, LANE), lambda i: (0, 0)))
        args.append(w_router)
    return pl.pallas_call(
        body,
        grid=(n_rows // bm,),
        in_specs=in_specs,
        out_specs=pl.BlockSpec((bm, out_cols), lambda i: (i, 0)),
        out_shape=jax.ShapeDtypeStruct((n_rows, out_cols), out_dtype),
        compiler_params=_params(),
        name=name,
    )(*args)


def _mm_plain_body(x_ref, w_ref, o_ref):
    o_ref[...] = _dot(x_ref[...], w_ref[...]).astype(o_ref.dtype)


def _mm_plain(x, w, t, *, col_off=0, n_cols=None, out_dtype=BF16):
    rows, k = x.shape
    n_cols = w.shape[1] - col_off if n_cols is None else n_cols
    bm, bn = t.mm_rows, t.mm_cols
    off = col_off // bn
    return pl.pallas_call(
        _mm_plain_body,
        grid=(n_cols // bn, rows // bm),
        in_specs=[pl.BlockSpec((bm, k), lambda j, i: (i, 0)),
                  pl.BlockSpec((k, bn), lambda j, i: (0, off + j))],
        out_specs=pl.BlockSpec((bm, bn), lambda j, i: (i, j)),
        out_shape=jax.ShapeDtypeStruct((rows, n_cols), out_dtype),
        compiler_params=_params(),
        name="mm_plain",
    )(x, w)


def _rope_rotate(blk, lane):
    fwd = pltpu.roll(blk, LANE - HEAD_DIM // 4, 1)
    bwd = pltpu.roll(blk, HEAD_DIM // 4, 1)
    return jnp.where(lane % (HEAD_DIM // 2) < HEAD_DIM // 4, fwd, bwd)


def _mm_rope_body(x_ref, w_ref, cos_ref, sin_ref, o_ref, *, n_rope_blocks):
    j = pl.program_id(0)

    @pl.when(j < n_rope_blocks)
    def _():
        cos, sin = cos_ref[...], sin_ref[...]
        lane = lax.broadcasted_iota(jnp.int32, cos.shape, 1)
        pair = min(2 * HEAD_DIM, o_ref.shape[1])
        for c in range(o_ref.shape[1] // pair):
            acc = _dot(x_ref[...], w_ref[:, c * pair:(c + 1) * pair])
            for hd in range(pair // HEAD_DIM):
                blk = acc[:, hd * HEAD_DIM:(hd + 1) * HEAD_DIM]
                sl = slice(c * pair + hd * HEAD_DIM, c * pair + (hd + 1) * HEAD_DIM)
                o_ref[:, sl] = (blk * cos + _rope_rotate(blk, lane) * sin).astype(o_ref.dtype)

    @pl.when(j >= n_rope_blocks)
    def _():
        o_ref[...] = _dot(x_ref[...], w_ref[...]).astype(o_ref.dtype)


def _mm_rope(x, w, cos, sin, n_rope_cols, t):
    rows, k = x.shape
    n = w.shape[1]
    bm, bn = t.mm_rows, t.mm_cols
    return pl.pallas_call(
        functools.partial(_mm_rope_body, n_rope_blocks=n_rope_cols // bn),
        grid=(n // bn, rows // bm),
        in_specs=[pl.BlockSpec((bm, k), lambda j, i: (i, 0)),
                  pl.BlockSpec((k, bn), lambda j, i: (0, j)),
                  pl.BlockSpec((bm, HEAD_DIM), lambda j, i: (i, 0)),
                  pl.BlockSpec((bm, HEAD_DIM), lambda j, i: (i, 0))],
        out_specs=pl.BlockSpec((bm, bn), lambda j, i: (i, j)),
        out_shape=jax.ShapeDtypeStruct((rows, n), BF16),
        compiler_params=_params(),
        name="mm_rope",
    )(x, w, cos, sin)


def _mm_conv_in_body(x_ref, wb_ref, wc_ref, wv_ref, b_ref, u_ref):
    x = x_ref[...]
    b_ref[...] = _dot(x, wb_ref[...]).astype(b_ref.dtype)
    u_ref[...] = (_dot(x, wc_ref[...]) * _dot(x, wv_ref[...])).astype(u_ref.dtype)


def _mm_conv_in(x, w_in, t):
    rows, k = x.shape
    d = w_in.shape[1] // 3
    bm, bn = t.mm_rows, t.conv_cols
    nb = d // bn
    wspec = lambda part: pl.BlockSpec((k, bn), lambda j, i: (0, part * nb + j))
    ospec = pl.BlockSpec((bm, bn), lambda j, i: (i, j))
    return pl.pallas_call(
        _mm_conv_in_body,
        grid=(nb, rows // bm),
        in_specs=[pl.BlockSpec((bm, k), lambda j, i: (i, 0)), wspec(0), wspec(1), wspec(2)],
        out_specs=[ospec, ospec],
        out_shape=[jax.ShapeDtypeStruct((rows, d), BF16)] * 2,
        compiler_params=_params(),
        name="mm_conv_in",
    )(x, w_in, w_in, w_in)


def _mm_res_body(*refs, n_x, n_res, first_blocks, col_scale):
    x_refs = refs[:n_x]
    w_ref = refs[n_x]
    res_refs = refs[n_x + 1:n_x + 1 + n_res]
    mod_ref = refs[n_x + 1 + n_res]
    o_ref = refs[-1]

    def emit(part):
        acc = _dot(x_refs[part][...], w_ref[...])
        if col_scale:
            acc = acc * refs[n_x + 2 + n_res][...]
        o_ref[...] = res_refs[min(part, n_res - 1)][...] + mod_ref[...] * acc

    if n_x == 1:
        emit(0)
    else:
        i = pl.program_id(1)
        pl.when(i < first_blocks)(lambda: emit(0))
        pl.when(i >= first_blocks)(lambda: emit(1))


def _mm_res(xs, w, res, mods5, layer, k_gate, lat, t, *, col_scale=None, grouped=False):
    k = xs[0].shape[1]
    bm = t.mm_rows
    part_blocks = [x.shape[0] // bm for x in xs]
    first = part_blocks[0]
    row_maps = [lambda i: jnp.minimum(i, first - 1), lambda i: jnp.maximum(i - first, 0)]
    in_place = not isinstance(res, (list, tuple))
    res_parts = [res] if in_place else list(res)
    d = res_parts[0].shape[1]
    if grouped:
        kg = w.shape[1]
        bn = kg
        x_specs = [pl.BlockSpec((bm, kg), lambda j, i, rm=rm: (rm(i), j)) for rm in row_maps[:len(xs)]]
        w_spec = pl.BlockSpec((None, kg, kg), lambda j, i: (j, 0, 0))
    else:
        bn = t.res_cols
        x_specs = [pl.BlockSpec((bm, k), lambda j, i, rm=rm: (rm(i), 0)) for rm in row_maps[:len(xs)]]
        w_spec = pl.BlockSpec((k, bn), lambda j, i: (0, j))
    if in_place:
        res_specs = [pl.BlockSpec((bm, bn), lambda j, i: (i, j))]
    else:
        assert [r.shape[0] // bm for r in res_parts] == part_blocks
        res_specs = [pl.BlockSpec((bm, bn), lambda j, i, rm=rm: (rm(i), j)) for rm in row_maps[:len(res_parts)]]
    in_specs = x_specs + [w_spec] + res_specs + [_mod_spec(layer, k_gate, bn, lat, bm, col_map=lambda j: j)]
    args = [*xs, w, *res_parts, mods5]
    if col_scale is not None:
        in_specs.append(pl.BlockSpec((1, bn), lambda j, i: (0, j)))
        args.append(col_scale.reshape(1, d))
    n_rows = sum(part_blocks) * bm
    return pl.pallas_call(
        functools.partial(_mm_res_body, n_x=len(xs), n_res=len(res_parts), first_blocks=first,
                          col_scale=col_scale is not None),
        grid=(d // bn, sum(part_blocks)),
        in_specs=in_specs,
        out_specs=pl.BlockSpec((bm, bn), lambda j, i: (i, j)),
        out_shape=jax.ShapeDtypeStruct(res.shape if in_place else (n_rows, d), F32),
        input_output_aliases={len(xs) + 1: 0} if in_place else {},
        compiler_params=_params(),
        name="mm_res",
    )(*args)


_POOL_HALO = 16


def _pool_body(h_ref, *rest, seq, chunk, groups_per_col):
    o_ref, pad_ref = rest[-2:]
    cols = h_ref.shape[1]
    zeros = jnp.zeros((_POOL_HALO, cols), F32)
    pad_ref[pl.ds(0, _POOL_HALO), :] = zeros
    pad_ref[pl.ds(_POOL_HALO + seq, _POOL_HALO), :] = zeros
    pad_ref[pl.ds(_POOL_HALO, seq), :] = h_ref[...]
    win_rows = chunk + 2 * _POOL_HALO
    group = pl.program_id(1) // groups_per_col

    for g, w in enumerate(POOL_WINDOWS):
        @pl.when(group == g)
        def _(w=w):
            def body(c, carry):
                start = pl.multiple_of(c * chunk, chunk)
                win = pad_ref[pl.ds(start, win_rows), :]
                s, width = win, 1
                while width < w:
                    s = s + pltpu.roll(s, width, 0)
                    width *= 2
                lead = w // 2 - 1
                if lead:
                    s = pltpu.roll(s, win_rows - lead, 0)
                tok = start + lax.broadcasted_iota(jnp.int32, (chunk, 1), 0)
                cnt = (jnp.minimum(tok + w // 2, seq) - jnp.maximum(tok - w // 2, 0)).astype(F32)
                centre = slice(_POOL_HALO, _POOL_HALO + chunk)
                o_ref[pl.ds(start, chunk), :] = (s[centre] / cnt - win[centre]).astype(o_ref.dtype)
                return carry
            lax.fori_loop(0, seq // chunk, body, 0)


def _pool_call(h, seq, n_seq, row_off, t):
    rows, d = h.shape
    bc = t.pool_cols
    chunk = min(t.pool_chunk, seq)
    group_cols = d // len(POOL_WINDOWS)
    off = row_off // seq
    return pl.pallas_call(
        functools.partial(_pool_body, seq=seq, chunk=chunk, groups_per_col=group_cols // bc),
        grid=(n_seq, d // bc),
        in_specs=[pl.BlockSpec((seq, bc), lambda b, j: (off + b, j))],
        out_specs=pl.BlockSpec((seq, bc), lambda b, j: (b, j)),
        out_shape=jax.ShapeDtypeStruct((n_seq * seq, d), BF16),
        scratch_shapes=[pltpu.VMEM((seq + 2 * _POOL_HALO, bc), F32)],
        compiler_params=_params(),
        name="pool",
    )(h)


def _conv_gate_body(b_ref, u_ref, w_ref, *rest, seq):
    o_ref = rest[-1]
    u = u_ref[...].astype(F32)
    row = lax.broadcasted_iota(jnp.int32, (seq, 1), 0)
    prev = jnp.where(row == 0, 0.0, pltpu.roll(u, 1, 0))
    nxt = jnp.where(row == seq - 1, 0.0, pltpu.roll(u, seq - 1, 0))
    conv = prev * w_ref[0:1, :] + u * w_ref[1:2, :] + nxt * w_ref[2:3, :]
    o_ref[...] = (b_ref[...].astype(F32) * conv).astype(o_ref.dtype)


def _conv_gate_call(bg, u, conv_w, seq, n_seq, row_off, t):
    rows, d = bg.shape
    bc = t.gate_cols
    off = row_off // seq
    spec = lambda: pl.BlockSpec((seq, bc), lambda b, j: (off + b, j))
    return pl.pallas_call(
        functools.partial(_conv_gate_body, seq=seq),
        grid=(n_seq, d // bc),
        in_specs=[spec(), spec(), pl.BlockSpec((CONV_W, bc), lambda b, j: (0, j))],
        out_specs=pl.BlockSpec((seq, bc), lambda b, j: (b, j)),
        out_shape=jax.ShapeDtypeStruct((n_seq * seq, d), BF16),
        compiler_params=_params(),
        name="conv_gate",
    )(bg, u, conv_w)


def _lane_fold(pieces, op):
    cols = [p[:, c:c + LANE] for p in pieces for c in range(0, p.shape[1], LANE)]
    acc = cols[0]
    for col in cols[1:]:
        acc = op(acc, col)
    return acc


def _attn_body(*refs, n_band, group, heads, scale, has_bias, bias_per_head, has_sink):
    q_ref = refs[0]
    pos = 1
    k_band = refs[pos:pos + n_band]
    v_band = refs[pos + n_band:pos + 2 * n_band]
    pos += 2 * n_band
    kx_ref, vx_ref = refs[pos], refs[pos + 1]
    pos += 2
    bias_ref = sink_ref = None
    if has_bias:
        bias_ref = refs[pos]
        pos += 1
    if has_sink:
        sink_ref = refs[pos]
        pos += 1
    o_ref = refs[-1]
    kv_block = pl.program_id(2) if has_sink else 0

    q_heads = [(hh, hh * group + g) for hh in range(heads) for g in range(group)]
    head_cols = lambda h: slice(h * HEAD_DIM, (h + 1) * HEAD_DIM)

    all_scores = []
    for hh, qh in q_heads:
        q = q_ref[:, head_cols(qh)]
        scores = [_dot_nt(q, r[:, head_cols(hh)]) * scale for r in (*k_band, kx_ref)]
        if has_bias:
            tk = k_band[0].shape[0]
            for bi in range(n_band):
                bsl = slice(bi * tk, (bi + 1) * tk)
                bias = bias_ref[hh, :, bsl] if bias_per_head else bias_ref[0, :, bsl]
                scores[bi] = scores[bi] + bias
        all_scores.append(scores)

    all_probs = []
    for (hh, qh), scores in zip(q_heads, all_scores):
        m = _lane_fold(scores, jnp.maximum).max(axis=-1, keepdims=True)
        if has_sink:
            sink = sink_ref[kv_block * heads * group + qh]
            m = jnp.maximum(m, sink)
        probs = [jnp.exp2(s - m) for s in scores]
        denom = _lane_fold(probs, jnp.add).sum(axis=-1, keepdims=True)
        if has_sink:
            denom = denom + jnp.exp2(sink - m)
        all_probs.append(([p.astype(BF16) for p in probs], denom))

    for (hh, qh), (probs, denom) in zip(q_heads, all_probs):
        vals = [r[:, head_cols(hh)] for r in (*v_band, vx_ref)]
        out = _dot(probs[0], vals[0])
        for p, v in zip(probs[1:], vals[1:]):
            out = out + _dot(p, v)
        o_ref[:, head_cols(qh)] = (out / denom).astype(o_ref.dtype)


def _attn_call(qkv, kvx, *, n_seq, seq, tq, n_q_heads, group, heads, q_col, k_col, v_col,
               kx_col, vx_col, x_row_off, x_len, q_row_off=0, band=True, bias=None,
               bias_per_head=False, sink=None, head_major=False):
    nb = seq // tq
    n_kv_heads = n_q_heads // group
    n_hblk = n_kv_heads // heads
    qw, kw = heads * group * HEAD_DIM, heads * HEAD_DIM
    n_band = 3 if band else 0
    if head_major:
        grid = (nb, n_hblk, n_seq)
        ids = lambda j, h, b: (b, j, h)
    else:
        grid = (n_seq, nb, n_hblk)
        ids = lambda b, j, h: (b, j, h)

    def wrap(f):
        return lambda *g: f(*ids(*g))

    q_blk0, x_blk0 = q_row_off // tq, x_row_off // x_len
    q_spec = pl.BlockSpec((tq, qw), wrap(lambda b, j, h: (q_blk0 + b * nb + j, q_col // qw + h)))
    in_specs, args = [q_spec], [qkv]

    def band_spec(col, shift):
        return pl.BlockSpec((tq, kw), wrap(
            lambda b, j, h: (q_blk0 + b * nb + jnp.clip(j + shift, 0, nb - 1), col // kw + h)))

    if band:
        for col in (k_col, v_col):
            for shift in (-1, 0, 1):
                in_specs.append(band_spec(col, shift))
                args.append(qkv)
    for col in (kx_col, vx_col):
        in_specs.append(pl.BlockSpec((x_len, kw), wrap(lambda b, j, h, col=col: (x_blk0 + b, col // kw + h))))
        args.append(kvx)
    if bias is not None:
        variant = lambda j: jnp.where(j == 0, 0, jnp.where(j == nb - 1, 2, 1))
        hb = heads if bias_per_head else 1
        in_specs.append(pl.BlockSpec((hb, None, tq, 3 * tq), wrap(
            lambda b, j, h: (h if bias_per_head else 0, variant(j), 0, 0))))
        args.append(bias * LOG2E)
    if sink is not None:
        in_specs.append(pl.BlockSpec(memory_space=pltpu.SMEM))
        args.append(sink * LOG2E)
    if sink is not None and head_major:
        raise ValueError("sink lookup assumes the head block is grid axis 2")
    return pl.pallas_call(
        functools.partial(_attn_body, n_band=n_band, group=group, heads=heads, scale=HEAD_DIM ** -0.5 * LOG2E,
                          has_bias=bias is not None, bias_per_head=bias_per_head, has_sink=sink is not None),
        grid=grid,
        in_specs=in_specs,
        out_specs=pl.BlockSpec((tq, qw), wrap(lambda b, j, h: (b * nb + j, h))),
        out_shape=jax.ShapeDtypeStruct((n_seq * seq, n_q_heads * HEAD_DIM), BF16),
        compiler_params=_params(),
        name="attn",
    )(*args)


def _swa_mask_table(seq):
    nb = seq // Q_BLOCK
    tables = []
    for j in (0, 1, nb - 1):
        qpos = j * Q_BLOCK + np.arange(Q_BLOCK)
        kpos = (j - 1) * Q_BLOCK + np.arange(3 * Q_BLOCK)
        valid = (np.abs(qpos[:, None] - kpos[None, :]) <= WINDOW) & ((kpos >= 0) & (kpos < seq))[None, :]
        tables.append(np.where(valid, 0.0, NEG_INF).astype(np.float32))
    return jnp.asarray(np.stack(tables)[None])


def _na_bias_table(rel_bias, seq):
    rows = seq // GRID_W
    kh = min(NA_KH_MAX, rows)
    rb_rows = NA_ROWS_PER_BLOCK
    nrb = rows // rb_rows
    rl, c = np.divmod(np.arange(rb_rows * GRID_W), GRID_W)
    kidx = np.arange(3 * rb_rows * GRID_W)
    kb, krem = np.divmod(kidx, rb_rows * GRID_W)
    krl, kc = np.divmod(krem, GRID_W)
    valid_all = []
    for rb in (0, 1, nrb - 1):
        r = rb_rows * rb + rl
        kr = rb_rows * (rb - 1 + kb) + krl
        r0 = np.clip(r - kh // 2, 0, rows - kh)
        c0 = np.clip(c - NA_KW // 2, 0, GRID_W - NA_KW)
        valid_all.append((kr[None, :] >= r0[:, None]) & (kr[None, :] < r0[:, None] + kh)
                         & (kc[None, :] >= c0[:, None]) & (kc[None, :] < c0[:, None] + NA_KW)
                         & (kr[None, :] >= 0) & (kr[None, :] < rows))
    valid = np.stack(valid_all)
    n_h, n_dr, n_dc = rel_bias.shape
    edge = GRID_W - NA_KW
    padded = jnp.pad(rel_bias.astype(F32), ((0, 0), (0, 0), (edge, edge)), mode="edge")
    toep = jnp.stack([padded[:, :, GRID_W - 1 - ci:2 * GRID_W - 1 - ci] for ci in range(GRID_W)], axis=2)
    dr0 = NA_KH_MAX - 1 - rb_rows
    assert dr0 - (rb_rows - 1) >= 0 and dr0 + 3 * rb_rows - 1 < n_dr
    blocks = jnp.stack([toep[:, dr0 - q:dr0 - q + 3 * rb_rows] for q in range(rb_rows)], axis=1)
    tbl = blocks.transpose(0, 1, 3, 2, 4).reshape(n_h, rb_rows * GRID_W, 3 * rb_rows * GRID_W)
    return jnp.where(jnp.asarray(valid)[None], tbl[:, None], NEG_INF)


def _rope_tables(seq, n_lat_rows, n_rows):
    half = HEAD_DIM // 4
    inv_freq = ROPE_BASE ** (-jnp.arange(half, dtype=F32) / half)
    t = jnp.arange(n_lat_rows) % seq
    ang_r = (t // GRID_W).astype(F32)[:, None] * inv_freq[None, :]
    ang_c = (t % GRID_W).astype(F32)[:, None] * inv_freq[None, :]
    cos = jnp.concatenate([jnp.cos(ang_r)] * 2 + [jnp.cos(ang_c)] * 2, axis=-1)
    sin = jnp.concatenate([-jnp.sin(ang_r), jnp.sin(ang_r), -jnp.sin(ang_c), jnp.sin(ang_c)], axis=-1)
    pad = n_rows - n_lat_rows
    cos = jnp.concatenate([cos, jnp.ones((pad, HEAD_DIM), F32)])
    sin = jnp.concatenate([sin, jnp.zeros((pad, HEAD_DIM), F32)])
    return cos, sin


_DMA_UNROLL = 8
_GATHER_CHUNKS = 8
_SCATTER_CHUNKS = 4


def _pack_bf16_pair(left, right):
    bits = lambda v: lax.bitcast_convert_type(v.astype(BF16).astype(F32), jnp.uint32)
    return (bits(left) & jnp.uint32(0xFFFF0000)) | (bits(right) >> 16)


def _unpack_bf16_pair(p):
    as_bf16 = lambda v: lax.bitcast_convert_type(v, F32).astype(BF16)
    return as_bf16(p & jnp.uint32(0xFFFF0000)), as_bf16(p << 16)


def _expert_body(idx_ref, pos_ref, x_hbm, gate_ref, gain_ref, sh_ref, sc_ref,
                 wg_ref, wu_ref, wd_ref, ys_hbm, xs, ybuf, gsem, ssem, *, rows, n_steps):
    step = pl.program_id(0) * pl.num_programs(1) + pl.program_id(1)
    slot = step % 2

    next_base = jnp.minimum(step + 1, n_steps - 1) * rows
    prev_base, own_base = step * rows, (step + 1) * rows

    def wait_gather(s):
        pltpu.make_async_copy(x_hbm.at[pl.ds(0, rows), :], xs.at[s], gsem.at[s]).wait()

    def wait_scatter(s):
        pltpu.make_async_copy(ybuf.at[s], ys_hbm.at[pl.ds(0, rows), :], ssem.at[s]).wait()

    @pl.when(step == 0)
    def _():
        def body(c, carry):
            pltpu.make_async_copy(x_hbm.at[pl.ds(idx_ref[c], 1), :], xs.at[0, pl.ds(c, 1), :], gsem.at[0]).start()
            return carry
        lax.fori_loop(0, rows, body, 0, unroll=_DMA_UNROLL)
        ybuf[1] = jnp.zeros(ybuf.shape[1:], ybuf.dtype)

    wait_gather(slot)
    h2 = _norm_mod(xs[slot], gain_ref, sh_ref, sc_ref).astype(BF16)
    d = h2.shape[1]
    kw, per = d // _GATHER_CHUNKS, rows // _GATHER_CHUNKS
    hg = hu = None
    for kc in range(_GATHER_CHUNKS):
        ksl = slice(kc * kw, (kc + 1) * kw)
        pg, pu = _dot(h2[:, ksl], wg_ref[ksl, :]), _dot(h2[:, ksl], wu_ref[ksl, :])
        hg, hu = (pg, pu) if hg is None else (hg + pg, hu + pu)
        for c in range(kc * per, (kc + 1) * per):
            pltpu.make_async_copy(x_hbm.at[pl.ds(idx_ref[next_base + c], 1), :],
                                  xs.at[1 - slot, pl.ds(c, 1), :], gsem.at[1 - slot]).start()
    hid = ((hg * jax.nn.sigmoid(hg)) * hu).astype(BF16)

    @pl.when(step >= 1)
    def _():
        wait_scatter(slot)

    half = d // 2
    cw, per = half // _SCATTER_CHUNKS, rows // _SCATTER_CHUNKS
    gate = gate_ref[...]
    for pc in range(_SCATTER_CHUNKS):
        lsl, rsl = slice(pc * cw, (pc + 1) * cw), slice(half + pc * cw, half + (pc + 1) * cw)
        ybuf[slot, :, lsl] = _pack_bf16_pair(_dot(hid, wd_ref[:, lsl]) * gate, _dot(hid, wd_ref[:, rsl]) * gate)
        for c in range(pc * per, (pc + 1) * per):
            pltpu.make_async_copy(ybuf.at[1 - slot, pl.ds(c, 1), :], ys_hbm.at[pl.ds(pos_ref[prev_base + c], 1), :],
                                  ssem.at[1 - slot]).start()

    @pl.when(step == n_steps - 1)
    def _():
        def scatter_body(c, carry):
            pltpu.make_async_copy(ybuf.at[slot, pl.ds(c, 1), :], ys_hbm.at[pl.ds(pos_ref[own_base + c], 1), :],
                                  ssem.at[slot]).start()
            return carry
        lax.fori_loop(0, rows, scatter_body, 0, unroll=_DMA_UNROLL)
        wait_gather(1 - slot)
        wait_scatter(1 - slot)
        wait_scatter(slot)


def _expert_call(xa, gain, mods5, layer, idx_rows, pos_rows, gate, wg, wu, wd, n_groups, n_out_rows, *,
                 n_lat_groups):
    _, n_exp, d, ff = wg.shape
    rows = idx_rows.shape[-1]
    n_steps = n_exp * n_groups
    spare = n_out_rows + jnp.arange(rows, dtype=jnp.int32).reshape(1, 1, rows)
    pos_rows = jnp.concatenate([spare, pos_rows], axis=0)
    n_out_rows += rows
    mod_spec = lambda k: pl.BlockSpec((None, None, None, 1, d),
                                      lambda e, g, *_: (layer, jnp.minimum(g, n_lat_groups), k, 0, 0))
    w_spec = lambda shape: pl.BlockSpec((None, None) + shape, lambda e, g, *_: (layer, e, 0, 0))
    return pl.pallas_call(
        functools.partial(_expert_body, rows=rows, n_steps=n_steps),
        grid_spec=pltpu.PrefetchScalarGridSpec(
            num_scalar_prefetch=2,
            grid=(n_exp, n_groups),
            in_specs=[pl.BlockSpec(memory_space=pl.ANY),
                      pl.BlockSpec((None, rows, 1), lambda e, g, *_: (e * n_groups + g, 0, 0)),
                      pl.BlockSpec((1, d), lambda e, g, *_: (0, 0)),
                      mod_spec(3), mod_spec(4),
                      w_spec((d, ff)), w_spec((d, ff)), w_spec((ff, d))],
            out_specs=pl.BlockSpec(memory_space=pl.ANY),
            scratch_shapes=[pltpu.VMEM((2, rows, d), F32), pltpu.VMEM((2, rows, d // 2), jnp.uint32),
                            pltpu.SemaphoreType.DMA((2,)), pltpu.SemaphoreType.DMA((2,))]),
        out_shape=jax.ShapeDtypeStruct((n_out_rows, d // 2), jnp.uint32),
        compiler_params=_params(),
        name="moe_expert",
    )(idx_rows.reshape(-1), pos_rows.reshape(-1), xa, gate, gain.reshape(1, d), mods5, mods5, wg, wu, wd)


_ITEM_FIRST, _ITEM_LAST, _ITEM_VALID = 1, 2, 4


def _combine_body(blk_ref, chunk_ref, flag_ref, tb_ref, modrow_ref, ys_ref, tok_ref, res_ref, mod_ref, *rest,
                  tokens, post):
    acc_ref = rest[-1]
    if post is None:
        o_ref = rest[-2]
    else:
        gain_ref, o_ref, h_ref = rest[0], rest[-3], rest[-2]
        if post == "norm_mod":
            sh_ref, sc_ref = rest[1], rest[2]
    t = pl.program_id(0)
    flag = flag_ref[t]

    @pl.when((flag & _ITEM_FIRST) != 0)
    def _():
        acc_ref[...] = jnp.zeros_like(acc_ref)

    @pl.when((flag & _ITEM_VALID) != 0)
    def _():
        pairs = tok_ref.shape[-1]
        tok = tb_ref[t] * tokens + lax.broadcasted_iota(jnp.int32, (tokens, pairs), 0)
        sel = (tok_ref[...] == tok).astype(BF16)
        y_left, y_right = _unpack_bf16_pair(ys_ref[...])
        half = y_left.shape[1]
        acc_ref[:, :half] += _dot(sel, y_left)
        acc_ref[:, half:] += _dot(sel, y_right)

    @pl.when((flag & _ITEM_LAST) != 0)
    def _():
        xn = res_ref[...] + mod_ref[...] * acc_ref[...]
        o_ref[...] = xn
        if post is not None:
            y = xn * lax.rsqrt(jnp.mean(xn * xn, axis=-1, keepdims=True) + EPS)
            y = y * gain_ref[...]
            if post == "norm_mod":
                y = y * (1 + sc_ref[...]) + sh_ref[...]
            h_ref[...] = y.astype(h_ref.dtype)


class _Post(NamedTuple):
    gain: jax.Array
    layer: int | None
    dtype: object
    n_rows: int


def _combine_call(ys, tok_sorted, items, res, mods5, layer, k_gate, t, post):
    d = res.shape[1]
    tokens = t.moe_tokens
    n_items = items[0].shape[0]
    n_prefetch = len(items)
    mod_spec = lambda lyr, k: pl.BlockSpec((None, None, None, 1, d),
                                           lambda i, blk, ch, fl, tb, mr: (lyr, mr[i], k, 0, 0))
    row_spec = lambda: pl.BlockSpec((tokens, d), lambda i, blk, ch, fl, tb, mr: (blk[i], 0))
    in_specs = [
        pl.BlockSpec((tokens, d // 2), lambda i, blk, ch, fl, tb, mr: (ch[i], 0)),
        pl.BlockSpec((None, 1, tokens), lambda i, blk, ch, fl, tb, mr: (ch[i], 0, 0)),
        row_spec(), mod_spec(layer, k_gate),
        pl.BlockSpec((1, d), lambda i, blk, ch, fl, tb, mr: (0, 0))]
    args = [*items, ys, tok_sorted, res, mods5, post.gain.reshape(1, d)]
    mode = "norm" if post.layer is None else "norm_mod"
    if mode == "norm_mod":
        in_specs += [mod_spec(post.layer, 0), mod_spec(post.layer, 1)]
        args += [mods5, mods5]
    return pl.pallas_call(
        functools.partial(_combine_body, tokens=tokens, post=mode),
        grid_spec=pltpu.PrefetchScalarGridSpec(
            num_scalar_prefetch=n_prefetch, grid=(n_items,), in_specs=in_specs,
            out_specs=[row_spec(), row_spec()],
            scratch_shapes=[pltpu.VMEM((tokens, d), F32)]),
        out_shape=[jax.ShapeDtypeStruct(res.shape, F32), jax.ShapeDtypeStruct((post.n_rows, d), post.dtype)],
        input_output_aliases={n_prefetch + 2: 0},
        compiler_params=_params(),
        name="moe_combine",
    )(*args)


def _route(aff, n_sets, set_len, n_exp, row_off, pair_off, tokens):
    cap = EC_CAPACITY_FACTOR * set_len // n_exp
    n_pairs = n_exp * cap
    aff = aff[row_off:row_off + n_sets * set_len, :n_exp].reshape(n_sets, set_len, n_exp)
    gate, idx = lax.top_k(aff.transpose(0, 2, 1), cap)
    flat = idx.reshape(n_sets, n_pairs)
    order = jnp.argsort(flat, axis=1)
    tok_sorted = jnp.take_along_axis(flat, order, axis=1)
    pos = jnp.argsort(order, axis=1)
    set_id = jnp.arange(n_sets, dtype=jnp.int32)[:, None]
    idx_rows = (row_off + set_id * set_len + flat).reshape(n_sets, n_exp, cap).astype(jnp.int32)
    pos_rows = (pair_off + set_id * n_pairs + pos).reshape(n_sets, n_exp, cap).astype(jnp.int32)

    n_blk, n_chunks = set_len // tokens, n_pairs // tokens
    bounds = jnp.arange(n_blk + 1, dtype=jnp.int32) * tokens
    start = jnp.sum(tok_sorted[:, :, None] < bounds[None, None, :], axis=1).astype(jnp.int32)
    lo, hi = start[:, :-1], start[:, 1:]
    first_chunk = jnp.minimum(lo // tokens, n_chunks - 1)
    last_chunk = jnp.where(hi > lo, (hi - 1) // tokens, first_chunk)
    count = last_chunk - first_chunk + 1
    end = jnp.cumsum(count, axis=1)
    begin = end - count
    n_items = n_chunks + n_blk - 1
    item = jnp.arange(n_items, dtype=jnp.int32)
    tb = jnp.minimum(jnp.sum(end[:, None, :] <= item[None, :, None], axis=2), n_blk - 1).astype(jnp.int32)
    pick = lambda a: jnp.take_along_axis(a, tb, axis=1)
    valid = item[None, :] < end[:, -1:]
    chunk = jnp.minimum(pick(first_chunk) + item[None, :] - pick(begin), pick(last_chunk))
    is_first = valid & (item[None, :] == pick(begin))
    is_last = valid & (item[None, :] == pick(end) - 1)
    flag = (is_first * _ITEM_FIRST + is_last * _ITEM_LAST + valid * _ITEM_VALID).astype(jnp.int32)
    set_col = set_id.astype(jnp.int32)
    blk = row_off // tokens + set_col * n_blk + tb
    chunk = pair_off // tokens + set_col * n_chunks + chunk.astype(jnp.int32)
    items = (blk, chunk, flag, tb)
    return (idx_rows, pos_rows, gate), tok_sorted.astype(jnp.int32).reshape(n_sets * n_chunks, 1, tokens), items


def _expert_groups(per_stream, rows, n_real_pairs):
    cols = ([], [], [])
    n_spare = 0
    for idx, pos, gate in per_stream:
        n_sets, n_exp, cap = idx.shape
        per_exp = n_sets * cap
        pad = -per_exp % rows
        fills = (None, None, None)
        if pad:
            spare = n_real_pairs + n_spare + (jnp.arange(n_exp, dtype=jnp.int32)[:, None] * pad
                                              + jnp.arange(pad, dtype=jnp.int32)[None, :])
            n_spare += n_exp * pad
            fills = (jnp.broadcast_to(idx[0, 0, 0], (n_exp, pad)), spare, jnp.zeros((n_exp, pad), gate.dtype))
        for col, arr, fill in zip(cols, (idx, pos, gate), fills):
            flat = arr.transpose(1, 0, 2).reshape(n_exp, per_exp)
            if pad:
                flat = jnp.concatenate([flat, fill], axis=1)
            col.append(flat.reshape(n_exp, -1, rows))
    idx, pos, gate = (jnp.concatenate(col, axis=1) for col in cols)
    n_groups = idx.shape[1]
    return (idx.reshape(-1, 1, rows), pos.reshape(-1, 1, rows), gate.reshape(-1, rows, 1), n_groups,
            n_real_pairs + n_spare)


def _moe(xa, aff, streams, gain2, moe_w, mods5, layer, t, post):
    wg, wu, wd = moe_w
    n_exp = wg.shape[1]
    n_lat_sets = streams[0][0]
    routed, pair_off = [], 0
    for n_sets, set_len, row_off in streams:
        routed.append(_route(aff, n_sets, set_len, n_exp, row_off, pair_off, t.moe_tokens))
        pair_off += n_sets * n_exp * (EC_CAPACITY_FACTOR * set_len // n_exp)
    rows = EC_CAPACITY_FACTOR * streams[0][1] // n_exp
    idx_rows, pos_rows, gate, n_groups, n_out = _expert_groups([r[0] for r in routed], rows, pair_off)
    ys = _expert_call(xa, gain2, mods5, layer, idx_rows, pos_rows, gate, wg, wu, wd, n_groups, n_out,
                      n_lat_groups=n_lat_sets)
    items = [jnp.concatenate([r[2][k].reshape(-1) for r in routed]) for k in range(4)]
    mod_row = jnp.concatenate([
        jnp.repeat(jnp.arange(n_sets, dtype=jnp.int32) if si == 0 else jnp.full((n_sets,), n_lat_sets, jnp.int32),
                   r[2][0].shape[1])
        for si, ((n_sets, _, _), r) in enumerate(zip(streams, routed))])
    tok_sorted = jnp.concatenate([r[1] for r in routed], axis=0)
    return _combine_call(ys, tok_sorted, (*items, mod_row), xa, mods5, layer, 5, t, post)


def _forward(x, c, ctx, c_ctx, norm1_gain, norm2_gain, final_gain, w_ada, b_ada,
             pool_w, pool_scale, conv_w_in, conv_w, conv_w_out,
             swa_w_qkv, swa_w_o, swa_sink, na_w_qkv, na_w_o, na_rel_bias,
             moe_router, moe_w_gate, moe_w_up, moe_w_down, t=Tiles()):
    bsz, seq, d = x.shape
    n_ctx = ctx.shape[1]
    depth = w_ada.shape[0]
    n_exp = moe_router.shape[-1]
    n_lat, n_all = bsz * seq, bsz * (seq + n_ctx)
    n_heads = d // HEAD_DIM
    n_kv = n_heads // KV_GROUP
    kv_dim = n_kv * HEAD_DIM
    assert bsz + 1 <= MOD_ROWS and seq % t.mm_rows == 0 and n_all % t.mm_rows == 0

    cc = jnp.concatenate([c, c_ctx[None], jnp.zeros((MOD_ROWS - bsz - 1, d), F32)], axis=0)
    mods5 = _ada_call(cc, w_ada, b_ada, t).reshape(depth, MOD_ROWS, N_MOD, 1, d)
    w_router = jnp.pad(moe_router, ((0, 0), (0, 0), (0, LANE - n_exp))).astype(BF16)
    moe_w = (moe_w_gate.astype(BF16), moe_w_up.astype(BF16), moe_w_down.astype(BF16))
    lat_stream, ctx_stream = (bsz, seq, 0), (bsz, n_ctx, n_lat)
    lat = (seq, bsz)
    h_dtype = lambda layer: F32 if layer % 4 == 0 else BF16

    parts0 = [x.reshape(n_lat, d)] + ([ctx.reshape(bsz * n_ctx, d)] if depth > 1 else [])
    xa = h = None
    for i in range(depth):
        m, j = i % 4, i // 4
        need_ctx = i < depth - 1
        rows_out = n_all if need_ctx else n_lat
        streams = [lat_stream, ctx_stream] if need_ctx else [lat_stream]

        if i == 0:
            h_parts = [_norm_call(p, norm1_gain[0], mods5, 0, 0, 1, lat, t, n_rows=p.shape[0],
                                  out_dtype=h_dtype(0), row_off=off)
                       for p, (_, _, off) in zip(parts0, streams)]
            y_in = [_pool_call(hp, s_len, n_s, 0, t) for hp, (n_s, s_len, _) in zip(h_parts, streams)]
            xa = _mm_res(y_in, pool_w[j].astype(BF16), parts0, mods5, i, 2, lat, t,
                         col_scale=pool_scale[j], grouped=True)
        elif m == 0:
            y_in = [_pool_call(h, s_len, n_s, off, t) for n_s, s_len, off in streams]
            xa = _mm_res(y_in, pool_w[j].astype(BF16), xa, mods5, i, 2, lat, t,
                         col_scale=pool_scale[j], grouped=True)
        elif m == 1:
            bg, u = _mm_conv_in(h, conv_w_in[j].astype(BF16), t)
            y_in = [_conv_gate_call(bg, u, conv_w[j], s_len, n_s, off, t) for n_s, s_len, off in streams]
            xa = _mm_res(y_in, conv_w_out[j].astype(BF16), xa, mods5, i, 2, lat, t)
        else:
            if m == 2:
                cos, sin = _rope_tables(seq, n_lat, n_all)
                qkv = _mm_rope(h, swa_w_qkv[j].astype(BF16), cos, sin, d + kv_dim, t)
                common = dict(n_seq=bsz, n_q_heads=n_heads, group=KV_GROUP, heads=t.swa_kv_heads,
                              q_col=0, k_col=d, v_col=d + kv_dim, kx_col=d, vx_col=d + kv_dim,
                              x_row_off=n_lat, x_len=n_ctx, sink=swa_sink[j])
                lat_kw = dict(tq=Q_BLOCK, bias=_swa_mask_table(seq))
                w_o = swa_w_o[j]
            else:
                qkv = _mm_plain(h, na_w_qkv[j].astype(BF16), t)
                common = dict(n_seq=bsz, n_q_heads=n_heads, group=1, heads=t.na_heads,
                              q_col=0, k_col=d, v_col=2 * d, kx_col=d, vx_col=2 * d,
                              x_row_off=n_lat, x_len=n_ctx)
                lat_kw = dict(tq=NA_ROWS_PER_BLOCK * GRID_W, bias=_na_bias_table(na_rel_bias[j], seq),
                              bias_per_head=True, head_major=True)
                w_o = na_w_o[j]
            y_in = [_attn_call(qkv, qkv, seq=seq, **lat_kw, **common)]
            if need_ctx:
                y_in.append(_attn_call(qkv, qkv, seq=n_ctx, tq=n_ctx, q_row_off=n_lat, band=False, **common))
            xa = _mm_res(y_in, w_o.astype(BF16), xa, mods5, i, 2, lat, t)

        aff = _norm_call(xa, norm2_gain[i], mods5, i, 3, 4, lat, t, n_rows=rows_out,
                         w_router=w_router[i], n_experts=n_exp)
        post = (_Post(norm1_gain[i + 1], i + 1, h_dtype(i + 1), n_all) if need_ctx
                else _Post(final_gain, None, F32, n_lat))
        xa, h = _moe(xa, aff, streams, norm2_gain[i], moe_w, mods5, i, t, post)

    return h.reshape(bsz, seq, d)


def kernel(x, c, ctx, c_ctx, norm1_gain, norm2_gain, final_gain, w_ada, b_ada, pool_w, pool_scale, conv_w_in, conv_w, conv_w_out, swa_w_qkv, swa_w_o, swa_sink, na_w_qkv, na_w_o, na_rel_bias, moe_router, moe_w_gate, moe_w_up, moe_w_down):
    return _forward(x, c, ctx, c_ctx, norm1_gain, norm2_gain, final_gain, w_ada, b_ada,
                    pool_w, pool_scale, conv_w_in, conv_w, conv_w_out,
                    swa_w_qkv, swa_w_o, swa_sink, na_w_qkv, na_w_o, na_rel_bias,
                    moe_router, moe_w_gate, moe_w_up, moe_w_down)
```
